```python
import math
import jax, jax.numpy as jnp
from jax import lax
import numpy as np

D_MODEL = 2048
BATCH = 2
SEQ = 4096
DEPTH = 4
DEC_BATCH = 8
DEC_SEQ = 4
PAST_LEN = 16384
PAGE_SIZE = 128

N_META = 16
N_MIXERS = 2
A_HEADS = 8
A_HD = 128
A_QK = A_HEADS * 2 * A_HD
B_HEADS = 16
B_KV_HEADS = 4
B_GROUP = B_HEADS // B_KV_HEADS
B_HD = 128
B_Q = B_HEADS * B_HD
B_KV = B_KV_HEADS * B_HD
IDX_HEADS = 16
IDX_HD = 128
I_Q = IDX_HEADS * IDX_HD
B_IN = B_Q + 2 * B_KV + I_Q + IDX_HD + IDX_HEADS
TOPK_MAX = 256
N_BUCKETS = 32
MAX_DISTANCE = 128
N_BIAS_HEADS = 16
D_FF = 4 * D_MODEL
Q_BLOCK = 128
EPS = 1e-6

kernel_name = "hybrid_diffattn_dsa_decoder_step"


def rmsnorm(x, g):
    xf = x.astype(jnp.float32)
    xf = xf * lax.rsqrt(jnp.mean(xf * xf, axis=-1, keepdims=True) + EPS)
    return (xf * g.astype(jnp.float32)).astype(x.dtype)


def rel_bucket(dist):
    max_exact = N_BUCKETS // 2
    n = jnp.maximum(dist, 0)
    nf = jnp.maximum(n, 1).astype(jnp.float32)
    large = max_exact + (jnp.log(nf / max_exact) / math.log(MAX_DISTANCE / max_exact)
                         * (N_BUCKETS - max_exact)).astype(jnp.int32)
    return jnp.where(n < max_exact, n, jnp.minimum(large, N_BUCKETS - 1))


def rel_bias(table, qpos, kpos):
    return table[rel_bucket(qpos - kpos)].astype(jnp.float32)


def sq_relu_mlp(h, w1, w2):
    return jnp.square(jax.nn.relu(h @ w1)) @ w2


def diff_project(h, w_in, q_gain, k_gain):
    B, S, _ = h.shape
    z = h @ w_in
    q = rmsnorm(z[..., :A_QK].reshape(B, S, A_HEADS, 2, A_HD), q_gain)
    k = rmsnorm(z[..., A_QK:2 * A_QK].reshape(B, S, A_HEADS, 2, A_HD), k_gain)
    v = z[..., 2 * A_QK:].reshape(B, S, A_HEADS, 2 * A_HD)
    return q, k, v


def diff_lambda(lam_p, lam_init):
    lp = lam_p.astype(jnp.float32)
    return jnp.exp(jnp.sum(lp[0] * lp[1])) - jnp.exp(jnp.sum(lp[2] * lp[3])) + lam_init


def diff_scores(q, k):
    return jnp.einsum("bqhmd,bkhmd->bhmqk", q, k,
                      preferred_element_type=jnp.float32) * (A_HD ** -0.5)


def diff_bias(table, qpos, kpos):
    b = rel_bias(table, qpos[:, None], kpos[None, :])
    return jnp.moveaxis(b, -1, 0).reshape(A_HEADS, 2, qpos.shape[0], kpos.shape[0])


def diff_probs(s, lam):
    p = jax.nn.softmax(s, axis=-1)
    return p[:, :, 0] - lam * p[:, :, 1]


def diff_output(o, subln, lam_init, w_o):
    B, S = o.shape[:2]
    o = rmsnorm(o, subln) * (1.0 - lam_init)
    return o.reshape(B, S, A_HEADS * 2 * A_HD) @ w_o


def diff_attn_prompt(h, w_in, w_o, q_gain, k_gain, lam_p, subln, table, lam_init):
    B, T, _ = h.shape
    q, k, v = diff_project(h, w_in, q_gain, k_gain)
    lam = diff_lambda(lam_p, lam_init)
    kpos = jnp.arange(T)
    nb = T // Q_BLOCK
    q_blocks = jnp.moveaxis(q.reshape(B, nb, Q_BLOCK, A_HEADS, 2, A_HD), 1, 0)

    def one_block(args):
        qb, start = args
        qpos = start + jnp.arange(Q_BLOCK)
        s = diff_scores(qb, k) + diff_bias(table, qpos, kpos)
        s = jnp.where(kpos[None, :] <= qpos[:, None], s, -jnp.inf)
        a = diff_probs(s, lam).astype(v.dtype)
        return jnp.einsum("bhqk,bkhe->bqhe", a, v)

    o = lax.map(one_block, (q_blocks, jnp.arange(nb) * Q_BLOCK))
    o = jnp.moveaxis(o, 0, 1).reshape(B, T, A_HEADS, 2 * A_HD)
    return diff_output(o, subln, lam_init, w_o), k, v


def diff_attn_sample(h, cache_k, cache_v, page_table, layer, past_len,
                     w_in, w_o, q_gain, k_gain, lam_p, subln, table, lam_init):
    DB, S, _ = h.shape
    q, k, v = diff_project(h, w_in, q_gain, k_gain)
    lam = diff_lambda(lam_p, lam_init)
    kp = cache_k[layer, page_table].reshape(DB, past_len, A_HEADS, 2, A_HD)
    vp = cache_v[layer, page_table].reshape(DB, past_len, A_HEADS, 2 * A_HD)
    qpos = past_len + jnp.arange(S)
    kpos = jnp.arange(past_len + S)
    s = jnp.concatenate([diff_scores(q, kp), diff_scores(q, k)], axis=-1)
    s = s + diff_bias(table, qpos, kpos)
    s = jnp.where(kpos[None, :] <= qpos[:, None], s, -jnp.inf)
    a = diff_probs(s, lam).astype(v.dtype)
    o = (jnp.einsum("bhqk,bkhe->bqhe", a[..., :past_len], vp)
         + jnp.einsum("bhqk,bkhe->bqhe", a[..., past_len:], v))
    return diff_output(o, subln, lam_init, w_o), k, v


def dsa_project(h, w_in, q_gain, k_gain):
    B, S, _ = h.shape
    z = h @ w_in
    o1 = B_Q
    o2 = o1 + B_KV
    o3 = o2 + B_KV
    o4 = o3 + I_Q
    o5 = o4 + IDX_HD
    q = rmsnorm(z[..., :o1].reshape(B, S, B_KV_HEADS, B_GROUP, B_HD), q_gain)
    k = rmsnorm(z[..., o1:o2].reshape(B, S, B_KV_HEADS, B_HD), k_gain)
    v = z[..., o2:o3].reshape(B, S, B_KV_HEADS, B_HD)
    iq = z[..., o3:o4].reshape(B, S, IDX_HEADS, IDX_HD)
    ik = z[..., o4:o5]
    iw = z[..., o5:]
    return q, k, v, iq, ik, iw


def index_scores(iq, iw, ik):
    dots = jnp.einsum("bqhd,bkd->bqhk", iq, ik,
                      preferred_element_type=jnp.float32) * (IDX_HD ** -0.5)
    w = iw.astype(jnp.float32) * (IDX_HEADS ** -0.5)
    return jnp.einsum("bqh,bqhk->bqk", w, jax.nn.relu(dots))


def dsa_attend(q, ks, vs, qpos, sel, valid, table):
    B, Q = q.shape[:2]
    n_sel = sel.shape[-1]
    s = jnp.einsum("bqngd,bqknd->bngqk", q, ks,
                   preferred_element_type=jnp.float32) * (B_HD ** -0.5)
    bias = rel_bias(table, qpos[None, :, None], sel)
    bias = jnp.transpose(bias, (0, 3, 1, 2)).reshape(B, B_KV_HEADS, B_GROUP, Q, n_sel)
    s = jnp.where(valid[:, None, None], s + bias, -jnp.inf)
    p = jax.nn.softmax(s, axis=-1).astype(vs.dtype)
    o = jnp.einsum("bngqk,bqknd->bqngd", p, vs)
    return o.reshape(B, Q, B_Q)


def dsa_prompt(h, w_in, w_o, q_gain, k_gain, table, n_sel):
    B, T, _ = h.shape
    q, k, v, iq, ik, iw = dsa_project(h, w_in, q_gain, k_gain)
    kpos = jnp.arange(T)
    nb = T // Q_BLOCK
    bidx = jnp.arange(B)[:, None, None]

    def blk(a):
        return jnp.moveaxis(a.reshape((B, nb, Q_BLOCK) + a.shape[2:]), 1, 0)

    def one_block(args):
        qb, iqb, iwb, start = args
        qpos = start + jnp.arange(Q_BLOCK)
        score = index_scores(iqb, iwb, ik)
        score = jnp.where(kpos[None, None, :] <= qpos[None, :, None], score, -jnp.inf)
        _, sel = lax.top_k(score, n_sel)
        valid = sel <= qpos[None, :, None]
        return dsa_attend(qb, k[bidx, sel], v[bidx, sel], qpos, sel, valid, table)

    o = lax.map(one_block, (blk(q), blk(iq), blk(iw), jnp.arange(nb) * Q_BLOCK))
    o = jnp.moveaxis(o, 0, 1).reshape(B, T, B_Q)
    return o @ w_o, k, v, ik


def dsa_sample(h, cache_k, cache_v, cache_ki, page_table, layer, past_len,
               w_in, w_o, q_gain, k_gain, table, n_sel):
    DB, S, _ = h.shape
    q, k, v, iq, ik, iw = dsa_project(h, w_in, q_gain, k_gain)
    ikp = cache_ki[layer, page_table].reshape(DB, past_len, IDX_HD)
    qpos = past_len + jnp.arange(S)
    kpos = jnp.arange(past_len + S)
    score = jnp.concatenate([index_scores(iq, iw, ikp), index_scores(iq, iw, ik)], axis=-1)
    score = jnp.where(kpos[None, None, :] <= qpos[None, :, None], score, -jnp.inf)
    _, sel = lax.top_k(score, n_sel)
    valid = sel <= qpos[None, :, None]
    bidx = jnp.arange(DB)[:, None, None]
    sp = jnp.minimum(sel, past_len - 1)
    phys = page_table[bidx, sp // PAGE_SIZE]
    off = sp % PAGE_SIZE
    sn = jnp.clip(sel - past_len, 0, S - 1)
    in_past = (sel < past_len)[..., None, None]
    ks = jnp.where(in_past, cache_k[layer, phys, off], k[bidx, sn])
    vs = jnp.where(in_past, cache_v[layer, phys, off], v[bidx, sn])
    o = dsa_attend(q, ks, vs, qpos, sel, valid, table)
    return o @ w_o, k, v, ik


def setup_inputs(seed: int = 0) -> dict:
    key = jax.random.key(seed)
    ks = jax.random.split(key, 32)
    n_a = (DEPTH + 1) // 2
    n_b = DEPTH // 2
    n_pages = PAST_LEN // PAGE_SIZE
    n_used = DEC_BATCH * n_pages
    n_pool = n_used + n_used // 4
    f32 = jnp.float32

    def nrm(k, shape, scale=1.0):
        return jax.random.normal(k, shape, f32) * scale

    def gain(k, shape):
        return 1.0 + 0.1 * jax.random.normal(k, shape, f32)

    page_table = jax.random.permutation(ks[7], n_pool)[:n_used].reshape(DEC_BATCH, n_pages).astype(jnp.int32)
    return {
        "x_prompt": nrm(ks[0], (BATCH, SEQ, D_MODEL)),
        "x_sample": nrm(ks[1], (DEC_BATCH, DEC_SEQ, D_MODEL)),
        "cache_a_k": nrm(ks[2], (n_a, n_pool, PAGE_SIZE, A_HEADS, 2, A_HD)),
        "cache_a_v": nrm(ks[3], (n_a, n_pool, PAGE_SIZE, A_HEADS, 2 * A_HD)),
        "cache_b_k": nrm(ks[4], (n_b, n_pool, PAGE_SIZE, B_KV_HEADS, B_HD)),
        "cache_b_v": nrm(ks[5], (n_b, n_pool, PAGE_SIZE, B_KV_HEADS, B_HD)),
        "cache_b_kidx": nrm(ks[6], (n_b, n_pool, PAGE_SIZE, IDX_HD)),
        "page_table": page_table,
        "meta_tokens": nrm(ks[8], (N_META, D_MODEL)),
        "rel_bias_table": nrm(ks[9], (N_BUCKETS, N_BIAS_HEADS), 0.5),
        "norm_mix": gain(ks[10], (DEPTH, D_MODEL)),
        "norm_ffn": gain(ks[11], (DEPTH, D_MODEL)),
        "a_w_in": nrm(ks[12], (n_a, D_MODEL, 3 * A_QK), D_MODEL ** -0.5),
        "a_w_out": nrm(ks[13], (n_a, A_QK, D_MODEL), A_QK ** -0.5),
        "a_q_norm": gain(ks[14], (n_a, A_HD)),
        "a_k_norm": gain(ks[15], (n_a, A_HD)),
        "a_lambda": nrm(ks[16], (n_a, 4, A_HD), 0.1),
        "a_subln": gain(ks[17], (n_a, 2 * A_HD)),
        "b_w_in": nrm(ks[18], (n_b, D_MODEL, B_IN), D_MODEL ** -0.5),
        "b_w_out": nrm(ks[19], (n_b, B_Q, D_MODEL), B_Q ** -0.5),
        "b_q_norm": gain(ks[20], (n_b, B_HD)),
        "b_k_norm": gain(ks[21], (n_b, B_HD)),
        "ffn_w1": nrm(ks[22], (DEPTH, D_MODEL, D_FF), D_MODEL ** -0.5),
        "ffn_w2": nrm(ks[23], (DEPTH, D_FF, D_MODEL), D_FF ** -0.5),
    }


def reference(x_prompt, x_sample, cache_a_k, cache_a_v, cache_b_k, cache_b_v, cache_b_kidx,
              page_table, meta_tokens, rel_bias_table, norm_mix, norm_ffn,
              a_w_in, a_w_out, a_q_norm, a_k_norm, a_lambda, a_subln,
              b_w_in, b_w_out, b_q_norm, b_k_norm, ffn_w1, ffn_w2):
    B, S, D = x_prompt.shape
    DS = x_sample.shape[1]
    past_len = page_table.shape[1] * PAGE_SIZE
    T = S + N_META
    T_pad = -(-T // Q_BLOCK) * Q_BLOCK
    meta = jnp.broadcast_to(meta_tokens.astype(x_prompt.dtype), (B, N_META, D))
    xp = jnp.concatenate([meta, x_prompt, jnp.zeros((B, T_pad - T, D), x_prompt.dtype)], axis=1)
    xs = x_sample
    n_sel_prompt = min(TOPK_MAX, S // 4)
    n_sel_sample = min(TOPK_MAX, (past_len + DS) // 4)

    a_k_p, a_v_p, a_k_s, a_v_s = [], [], [], []
    b_k_p, b_v_p, b_i_p, b_k_s, b_v_s, b_i_s = [], [], [], [], [], []
    for i in range(DEPTH):
        j = i // N_MIXERS
        hp = rmsnorm(xp, norm_mix[i])
        hs = rmsnorm(xs, norm_mix[i])
        if i % N_MIXERS == 0:
            lam_init = 0.8 - 0.6 * math.exp(-0.3 * i)
            yp, kp, vp = diff_attn_prompt(hp, a_w_in[j], a_w_out[j], a_q_norm[j], a_k_norm[j],
                                          a_lambda[j], a_subln[j], rel_bias_table, lam_init)
            ys, kn, vn = diff_attn_sample(hs, cache_a_k, cache_a_v, page_table, j, past_len,
                                          a_w_in[j], a_w_out[j], a_q_norm[j], a_k_norm[j],
                                          a_lambda[j], a_subln[j], rel_bias_table, lam_init)
            a_k_p.append(kp[:, :T]); a_v_p.append(vp[:, :T])
            a_k_s.append(kn); a_v_s.append(vn)
        else:
            yp, kp, vp, ip = dsa_prompt(hp, b_w_in[j], b_w_out[j], b_q_norm[j], b_k_norm[j],
                                        rel_bias_table, n_sel_prompt)
            ys, kn, vn, inn = dsa_sample(hs, cache_b_k, cache_b_v, cache_b_kidx, page_table, j, past_len,
                                         b_w_in[j], b_w_out[j], b_q_norm[j], b_k_norm[j],
                                         rel_bias_table, n_sel_sample)
            b_k_p.append(kp[:, :T]); b_v_p.append(vp[:, :T]); b_i_p.append(ip[:, :T])
            b_k_s.append(kn); b_v_s.append(vn); b_i_s.append(inn)
        xp = xp + yp
        xs = xs + ys
        xp = xp + sq_relu_mlp(rmsnorm(xp, norm_ffn[i]), ffn_w1[i], ffn_w2[i])
        xs = xs + sq_relu_mlp(rmsnorm(xs, norm_ffn[i]), ffn_w1[i], ffn_w2[i])

    y_prompt = xp[:, N_META:T]
    return (y_prompt, xs,
            jnp.stack(a_k_p), jnp.stack(a_v_p), jnp.stack(b_k_p), jnp.stack(b_v_p), jnp.stack(b_i_p),
            jnp.stack(a_k_s), jnp.stack(a_v_s), jnp.stack(b_k_s), jnp.stack(b_v_s), jnp.stack(b_i_s))
```

```python
import functools
import math

import jax
import jax.numpy as jnp
from jax import lax
from jax.experimental import pallas as pl
from jax.experimental.pallas import tpu as pltpu

F32 = jnp.float32
BF16 = jnp.bfloat16
I32 = jnp.int32

N_META = 16
N_MIXERS = 2
A_HEADS = 8
A_HD = 128
A_QK = A_HEADS * 2 * A_HD
B_HEADS = 16
B_KV_HEADS = 4
B_GROUP = B_HEADS // B_KV_HEADS
B_HD = 128
B_Q = B_HEADS * B_HD
B_KV = B_KV_HEADS * B_HD
IDX_HEADS = 16
IDX_HD = 128
I_Q = IDX_HEADS * IDX_HD
B_IN = B_Q + 2 * B_KV + I_Q + IDX_HD + IDX_HEADS
TOPK_MAX = 256
N_BUCKETS = 32
MAX_DISTANCE = 128
N_BIAS_HEADS = 16
PAGE_SIZE = 128
Q_BLOCK = 128
EPS = 1e-6

LANES = 128
SUBLANES = 8
ATT_TILE = 384
TAIL_ROWS = 256
NEG = -1e30
INT_MIN = -2 ** 31
SAMPLE_PAGES_PER_STEP = 4
VMEM_LIMIT_BYTES = 48 * 1024 * 1024

_NT = (((1,), (1,)), ((), ()))


def _cparams(sem):
    return pltpu.CompilerParams(dimension_semantics=sem, vmem_limit_bytes=VMEM_LIMIT_BYTES)


def _pick(n, candidates):
    for c in candidates:
        if n % c == 0:
            return c
    raise ValueError(f"no tile for {n} in {candidates}")


def _rms_kernel(x_ref, g_ref, o_ref):
    x = x_ref[...]
    ms = jnp.mean(x * x, axis=-1, keepdims=True)
    o_ref[...] = (x * lax.rsqrt(ms + EPS) * g_ref[...]).astype(o_ref.dtype)


def _rmsnorm(x, g):
    m, d = x.shape
    tm = _pick(m, (512, 256))
    return pl.pallas_call(
        _rms_kernel,
        grid=(m // tm,),
        in_specs=[pl.BlockSpec((tm, d), lambda i: (i, 0)),
                  pl.BlockSpec((1, d), lambda i: (0, 0))],
        out_specs=pl.BlockSpec((tm, d), lambda i: (i, 0)),
        out_shape=jax.ShapeDtypeStruct((m, d), BF16),
        compiler_params=_cparams(("parallel",)),
        name="rmsnorm",
    )(x, g.reshape(1, d).astype(F32))


def _mm_kernel(*refs, nk, mode):
    if mode == "res":
        a_ref, w_ref, r_ref, o_ref, acc_ref = refs
    else:
        a_ref, w_ref, o_ref, acc_ref = refs
        r_ref = None
    k = pl.program_id(2)

    @pl.when(k == 0)
    def _():
        acc_ref[...] = jnp.zeros_like(acc_ref)

    acc_ref[...] += jnp.dot(a_ref[...], w_ref[...], preferred_element_type=F32)

    @pl.when(k == nk - 1)
    def _():
        acc = acc_ref[...]
        if mode == "relu2":
            r = jnp.maximum(acc, 0.0)
            acc = r * r
        elif mode == "res":
            acc = acc + r_ref[...]
        o_ref[...] = acc.astype(o_ref.dtype)


def _matmul(a, w, *, mode="plain", res=None, out_dtype=F32):
    m, kd = a.shape
    n = w.shape[1]
    tm = _pick(m, (512, 256))
    tn = _pick(n, (1024, 768, 512, 256, 128))
    tk = _pick(kd, (2048, 1024, 512, 256))
    nk = kd // tk
    in_specs = [pl.BlockSpec((tm, tk), lambda i, j, k: (i, k)),
                pl.BlockSpec((tk, tn), lambda i, j, k: (k, j))]
    args = [a, w]
    if mode == "res":
        in_specs.append(pl.BlockSpec((tm, tn), lambda i, j, k: (i, j)))
        args.append(res)
    return pl.pallas_call(
        functools.partial(_mm_kernel, nk=nk, mode=mode),
        grid=(m // tm, n // tn, nk),
        in_specs=in_specs,
        out_specs=pl.BlockSpec((tm, tn), lambda i, j, k: (i, j)),
        out_shape=jax.ShapeDtypeStruct((m, n), out_dtype),
        scratch_shapes=[pltpu.VMEM((tm, tn), F32)],
        compiler_params=_cparams(("parallel", "parallel", "arbitrary")),
        name="matmul_" + mode,
    )(*args)


def _bias_kernel(tab_ref, d_ref, o_ref):
    h = pl.program_id(0)
    d = d_ref[...]
    max_exact = N_BUCKETS // 2
    n = jnp.maximum(d, 0)
    nf = jnp.maximum(n, 1).astype(F32)
    large = max_exact + (jnp.log(nf / max_exact) / math.log(MAX_DISTANCE / max_exact)
                         * (N_BUCKETS - max_exact)).astype(I32)
    bucket = jnp.where(n < max_exact, n, jnp.minimum(large, N_BUCKETS - 1))
    acc = jnp.zeros(d.shape, F32)
    for b in range(N_BUCKETS):
        acc = jnp.where(bucket == b, tab_ref[b * N_BIAS_HEADS + h], acc)
    o_ref[0] = jnp.where(d < 0, NEG, acc)


def _bias_tiles(table, dist):
    r, c = dist.shape
    return pl.pallas_call(
        _bias_kernel,
        grid=(N_BIAS_HEADS,),
        in_specs=[pl.BlockSpec(memory_space=pltpu.SMEM),
                  pl.BlockSpec((r, c), lambda h: (0, 0))],
        out_specs=pl.BlockSpec((1, r, c), lambda h: (h, 0, 0)),
        out_shape=jax.ShapeDtypeStruct((N_BIAS_HEADS, r, c), F32),
        compiler_params=_cparams(("arbitrary",)),
        name="bias_tiles",
    )(table.astype(F32).reshape(-1), dist)


def _head_rms(x, g):
    ms = jnp.mean(x * x, axis=-1, keepdims=True)
    return x * lax.rsqrt(ms + EPS) * g


def _a_post_kernel(zq_ref, zk_ref, zv_ref, qg_ref, kg_ref, q_ref, k_ref, kb_ref, vb_ref):
    scale = A_HD ** -0.5
    for g in range(2 * A_HEADS):
        sl = slice(g * A_HD, (g + 1) * A_HD)
        q_ref[:, sl] = (_head_rms(zq_ref[:, sl], qg_ref[...]) * scale).astype(BF16)
        kn = _head_rms(zk_ref[:, sl], kg_ref[...])
        k_ref[:, sl] = kn
        kb_ref[:, sl] = kn.astype(BF16)
    vb_ref[...] = zv_ref[...].astype(BF16)


def _a_post(z, q_gain, k_gain):
    m = z.shape[0]
    tm = 256
    col = lambda c: pl.BlockSpec((tm, A_QK), lambda i, c=c: (i, c))
    gain = pl.BlockSpec((1, A_HD), lambda i: (0, 0))
    out = pl.BlockSpec((tm, A_QK), lambda i: (i, 0))
    return pl.pallas_call(
        _a_post_kernel,
        grid=(m // tm,),
        in_specs=[col(0), col(1), col(2), gain, gain],
        out_specs=[out, out, out, out],
        out_shape=[jax.ShapeDtypeStruct((m, A_QK), BF16), jax.ShapeDtypeStruct((m, A_QK), F32),
                   jax.ShapeDtypeStruct((m, A_QK), BF16), jax.ShapeDtypeStruct((m, A_QK), BF16)],
        compiler_params=_cparams(("parallel",)),
        name="diff_post",
    )(z, z, z, q_gain.reshape(1, A_HD).astype(F32), k_gain.reshape(1, A_HD).astype(F32))


def _b_post_kernel(zq_ref, zk_ref, zv_ref, zi0_ref, zi1_ref, zik_ref, ziw_ref, qg_ref, kg_ref,
                   q_ref, k_ref, kb_ref, vb_ref, iq_ref, ikb_ref, iw_ref):
    for g in range(B_HEADS):
        sl = slice(g * B_HD, (g + 1) * B_HD)
        q_ref[:, sl] = (_head_rms(zq_ref[:, sl], qg_ref[...]) * (B_HD ** -0.5)).astype(BF16)
    for g in range(B_KV_HEADS):
        sl = slice(g * B_HD, (g + 1) * B_HD)
        kn = _head_rms(zk_ref[:, sl], kg_ref[...])
        k_ref[:, sl] = kn
        kb_ref[:, sl] = kn.astype(BF16)
    vb_ref[...] = zv_ref[...].astype(BF16)
    half = I_Q // 2
    iq_ref[:, :half] = (zi0_ref[...] * (IDX_HD ** -0.5)).astype(BF16)
    iq_ref[:, half:] = (zi1_ref[...] * (IDX_HD ** -0.5)).astype(BF16)
    ikb_ref[...] = zik_ref[...].astype(BF16)
    iw_ref[...] = ziw_ref[...] * (IDX_HEADS ** -0.5)


def _b_post(z, q_gain, k_gain):
    m = z.shape[0]
    tm = 256
    blk = lambda w, c: pl.BlockSpec((tm, w), lambda i, c=c: (i, c))
    gain = pl.BlockSpec((1, B_HD), lambda i: (0, 0))
    o1 = B_Q
    o2 = o1 + B_KV
    o3 = o2 + B_KV
    o4 = o3 + I_Q
    half = I_Q // 2
    out = lambda w: pl.BlockSpec((tm, w), lambda i: (i, 0))
    sds = lambda w, dt: jax.ShapeDtypeStruct((m, w), dt)
    return pl.pallas_call(
        _b_post_kernel,
        grid=(m // tm,),
        in_specs=[blk(B_Q, 0), blk(B_KV, o1 // B_KV), blk(B_KV, o2 // B_KV),
                  blk(half, o3 // half), blk(half, o3 // half + 1),
                  blk(IDX_HD, o4 // IDX_HD), blk(LANES, o4 // IDX_HD + 1), gain, gain],
        out_specs=[out(B_Q), out(B_KV), out(B_KV), out(B_KV), out(I_Q), out(IDX_HD), out(LANES)],
        out_shape=[sds(B_Q, BF16), sds(B_KV, F32), sds(B_KV, BF16), sds(B_KV, BF16),
                   sds(I_Q, BF16), sds(IDX_HD, BF16), sds(LANES, F32)],
        compiler_params=_cparams(("parallel",)),
        name="dsa_post",
    )(z, z, z, z, z, z, z, q_gain.reshape(1, B_HD).astype(F32), k_gain.reshape(1, B_HD).astype(F32))


def _softmax_update(s, v, m_ref, l_ref, acc_ref, idx):
    m_old = m_ref[idx]
    m_new = jnp.maximum(m_old, jnp.max(s, axis=1, keepdims=True))
    alpha = jnp.exp(m_old - m_new)
    p = jnp.exp(s - m_new)
    l_ref[idx] = alpha * l_ref[idx] + jnp.sum(p, axis=1, keepdims=True)
    acc_ref[idx] = alpha * acc_ref[idx] + jnp.dot(p.astype(BF16), v, preferred_element_type=F32)
    m_ref[idx] = m_new


def _diff_lambda(lp, lam_init):
    s1 = jnp.sum(lp[0:1] * lp[1:2], axis=1, keepdims=True)
    s2 = jnp.sum(lp[2:3] * lp[3:4], axis=1, keepdims=True)
    return jnp.exp(s1) - jnp.exp(s2) + lam_init


def _diff_attn_kernel(far_ref, q_ref, k_ref, v_ref, bias_ref, lamp_ref, sub_ref, o_ref,
                      m_ref, l_ref, acc_ref, *, lam_init, tq):
    h = pl.program_id(1)
    qi = pl.program_id(2)
    m_ref[...] = jnp.full(m_ref.shape, NEG, F32)
    l_ref[...] = jnp.zeros_like(l_ref)
    acc_ref[...] = jnp.zeros_like(acc_ref)

    def block(j, bias_of_map):
        off = pl.multiple_of(j * tq, tq)
        k = k_ref[pl.ds(off, tq), :]
        v = v_ref[pl.ds(off, tq), :]
        for mp in range(2):
            sl = slice(mp * A_HD, (mp + 1) * A_HD)
            s = lax.dot_general(q_ref[:, sl], k[:, sl], _NT, preferred_element_type=F32)
            _softmax_update(s + bias_of_map(mp), v, m_ref, l_ref, acc_ref, mp)

    def far_body(j, c):
        block(j, lambda mp: far_ref[2 * h + mp])
        return c

    lax.fori_loop(0, jnp.maximum(qi - 1, 0), far_body, 0)

    @pl.when(qi >= 1)
    def _():
        block(qi - 1, lambda mp: bias_ref[mp, tq:2 * tq, :])

    block(qi, lambda mp: bias_ref[mp, 0:tq, :])

    lam = _diff_lambda(lamp_ref[...], lam_init)
    o = acc_ref[0] / l_ref[0] - lam * (acc_ref[1] / l_ref[1])
    o_ref[...] = (_head_rms(o, sub_ref[...]) * (1.0 - lam_init)).astype(o_ref.dtype)


def _diff_attn_prompt(q, kb, vb, bias, far, lam_p, subln, *, batch, t_pad, lam_init):
    tq = ATT_TILE
    nq = t_pad // tq
    hw = 2 * A_HD
    return pl.pallas_call(
        functools.partial(_diff_attn_kernel, lam_init=lam_init, tq=tq),
        grid=(batch, A_HEADS, nq),
        in_specs=[pl.BlockSpec(memory_space=pltpu.SMEM),
                  pl.BlockSpec((tq, hw), lambda b, h, i: (b * nq + i, h)),
                  pl.BlockSpec((t_pad, hw), lambda b, h, i: (b, h)),
                  pl.BlockSpec((t_pad, hw), lambda b, h, i: (b, h)),
                  pl.BlockSpec((2, 2 * tq, tq), lambda b, h, i: (h, 0, 0)),
                  pl.BlockSpec((4, A_HD), lambda b, h, i: (0, 0)),
                  pl.BlockSpec((1, hw), lambda b, h, i: (0, 0))],
        out_specs=pl.BlockSpec((tq, hw), lambda b, h, i: (b * nq + i, h)),
        out_shape=jax.ShapeDtypeStruct((batch * t_pad, A_QK), BF16),
        scratch_shapes=[pltpu.VMEM((2, tq, 1), F32), pltpu.VMEM((2, tq, 1), F32),
                        pltpu.VMEM((2, tq, hw), F32)],
        compiler_params=_cparams(("parallel", "parallel", "arbitrary")),
        name="diff_attn_prompt",
    )(far, q, kb, vb, bias, lam_p.astype(F32), subln.reshape(1, hw).astype(F32))


def _sortable_key(x):
    bits = lax.bitcast_convert_type(x, I32)
    return bits ^ ((bits >> 31) & 0x7FFFFFFF)


def _kth_largest_key(count_ge, n_sel, shape):
    zero = jnp.zeros(shape, I32)
    t = jnp.where(count_ge(zero) >= n_sel, zero, jnp.full(shape, INT_MIN, I32))

    def bit_body(i, t):
        cand = t + jnp.left_shift(jnp.int32(1), 30 - i)
        return jnp.where(count_ge(cand) >= n_sel, cand, t)

    return lax.fori_loop(0, 31, bit_body, t)


def _dsa_kernel(far_ref, q_ref, iq_ref, iw_ref, k_ref, v_ref, ik_ref, bias_ref, o_ref,
                keys_ref, thr_ref, m_ref, l_ref, acc_ref, *, tq, n_sel):
    qi = pl.program_id(1)
    n = pl.program_id(2)

    @pl.when(n == 0)
    def _():
        row = lax.broadcasted_iota(I32, (tq, tq), 0)
        colm = lax.broadcasted_iota(I32, (tq, tq), 1)

        def score_body(j, c):
            off = pl.multiple_of(j * tq, tq)
            ikb = ik_ref[pl.ds(off, tq), :]
            acc = jnp.zeros((tq, tq), F32)
            for h in range(IDX_HEADS):
                d = lax.dot_general(iq_ref[:, h * IDX_HD:(h + 1) * IDX_HD], ikb, _NT,
                                    preferred_element_type=F32)
                acc = acc + jnp.maximum(d, 0.0) * iw_ref[:, h:h + 1]
            lim = jnp.where(j < qi, tq, 0)
            acc = jnp.where(colm <= row + lim, acc, -jnp.inf)
            keys_ref[:, pl.ds(off, tq)] = _sortable_key(acc)
            return c

        lax.fori_loop(0, qi + 1, score_body, 0)

        def count_ge(cand):
            def col_body(j, pc):
                blk = keys_ref[:, pl.ds(pl.multiple_of(j * tq, tq), tq)]
                ind = jnp.where(blk >= cand, 1, 0)
                for c in range(tq // LANES):
                    pc = pc + ind[:, c * LANES:(c + 1) * LANES]
                return pc
            pc = lax.fori_loop(0, qi + 1, col_body, jnp.zeros((tq, LANES), I32))
            return jnp.sum(pc, axis=1, keepdims=True)

        thr_ref[...] = _kth_largest_key(count_ge, n_sel, (tq, 1))

    m_ref[...] = jnp.full(m_ref.shape, NEG, F32)
    l_ref[...] = jnp.zeros_like(l_ref)
    acc_ref[...] = jnp.zeros_like(acc_ref)

    def block(j, bias_of_head):
        off = pl.multiple_of(j * tq, tq)
        k = k_ref[pl.ds(off, tq), :]
        v = v_ref[pl.ds(off, tq), :]
        sel = keys_ref[:, pl.ds(off, tq)] >= thr_ref[...]
        for g in range(B_GROUP):
            s = lax.dot_general(q_ref[:, g * B_HD:(g + 1) * B_HD], k, _NT, preferred_element_type=F32)
            s = jnp.where(sel, s + bias_of_head(g), NEG)
            _softmax_update(s, v, m_ref, l_ref, acc_ref, g)

    def far_body(j, c):
        block(j, lambda g: far_ref[n * B_GROUP + g])
        return c

    lax.fori_loop(0, jnp.maximum(qi - 1, 0), far_body, 0)

    @pl.when(qi >= 1)
    def _():
        block(qi - 1, lambda g: bias_ref[g, tq:2 * tq, :])

    block(qi, lambda g: bias_ref[g, 0:tq, :])

    for g in range(B_GROUP):
        o_ref[:, g * B_HD:(g + 1) * B_HD] = (acc_ref[g] / l_ref[g]).astype(o_ref.dtype)


def _dsa_prompt(q, iq, iw, kb, vb, ikb, bias, far, *, batch, t_pad, n_sel):
    tq = ATT_TILE
    nq = t_pad // tq
    gw = B_GROUP * B_HD
    return pl.pallas_call(
        functools.partial(_dsa_kernel, tq=tq, n_sel=n_sel),
        grid=(batch, nq, B_KV_HEADS),
        in_specs=[pl.BlockSpec(memory_space=pltpu.SMEM),
                  pl.BlockSpec((tq, gw), lambda b, i, n: (b * nq + i, n)),
                  pl.BlockSpec((tq, I_Q), lambda b, i, n: (b * nq + i, 0)),
                  pl.BlockSpec((tq, LANES), lambda b, i, n: (b * nq + i, 0)),
                  pl.BlockSpec((t_pad, B_HD), lambda b, i, n: (b, n)),
                  pl.BlockSpec((t_pad, B_HD), lambda b, i, n: (b, n)),
                  pl.BlockSpec((t_pad, IDX_HD), lambda b, i, n: (b, 0)),
                  pl.BlockSpec((B_GROUP, 2 * tq, tq), lambda b, i, n: (n, 0, 0))],
        out_specs=pl.BlockSpec((tq, gw), lambda b, i, n: (b * nq + i, n)),
        out_shape=jax.ShapeDtypeStruct((batch * t_pad, B_Q), BF16),
        scratch_shapes=[pltpu.VMEM((tq, t_pad), I32), pltpu.VMEM((tq, 1), I32),
                        pltpu.VMEM((B_GROUP, tq, 1), F32), pltpu.VMEM((B_GROUP, tq, 1), F32),
                        pltpu.VMEM((B_GROUP, tq, B_HD), F32)],
        compiler_params=_cparams(("parallel", "arbitrary", "arbitrary")),
        name="dsa_prompt",
    )(far, q, iq, iw, kb, vb, ikb, bias)


def _page_specs(comps, layer, n_pages, npp):
    def spec(i):
        return pl.BlockSpec((1, 1, PAGE_SIZE * comps, LANES),
                            lambda b, p, pt, i=i: (layer, pt[b * n_pages + p * npp + i], 0, 0))
    return [spec(i) for i in range(npp)]


def _component(ref, c, comps):
    return ref[0, 0, pl.ds(c, PAGE_SIZE, stride=comps), :].astype(BF16)


def _sample_attn_body(p, npg, npp, qbd, k_refs, v_refs, kcomps, vcomps, v_ids, knew_ref, vnew_ref,
                      bconst_ref, blast_ref, bnew_ref, m_ref, l_ref, acc_ref, mask_of):
    groups = len(v_ids)
    rg = LANES // groups
    vw = acc_ref.shape[1]

    @pl.when(p == 0)
    def _():
        m_ref[...] = jnp.full(m_ref.shape, NEG, F32)
        l_ref[...] = jnp.zeros_like(l_ref)
        acc_ref[...] = jnp.zeros_like(acc_ref)

    def update(xcat, v_group, bias, mask_idx):
        s = lax.dot_general(qbd, xcat, _NT, preferred_element_type=F32)
        s = mask_of(mask_idx, s + bias)
        m_old = m_ref[...]
        m_new = jnp.maximum(m_old, jnp.max(s, axis=1, keepdims=True))
        alpha = jnp.exp(m_old - m_new)
        pr = jnp.exp(s - m_new)
        l_ref[...] = alpha * l_ref[...] + jnp.sum(pr, axis=1, keepdims=True)
        m_ref[...] = m_new
        pb = pr.astype(BF16)
        for g in range(groups):
            rows = slice(g * rg, (g + 1) * rg)
            acc_ref[rows, :] = alpha[rows] * acc_ref[rows, :] + jnp.dot(
                pb[rows], v_group(g), preferred_element_type=F32)

    is_last = p == npg - 1
    for i in range(npp):
        xcat = jnp.concatenate([_component(k_refs[i], c, kcomps) for c in range(kcomps)], axis=1)

        def v_group(g, i=i):
            parts = [_component(v_refs[i], c, vcomps) for c in v_ids[g]]
            return parts[0] if len(parts) == 1 else jnp.concatenate(parts, axis=1)

        if i == npp - 1:
            w_last = jnp.where(is_last, 1.0, 0.0)
            bias = w_last * blast_ref[...] + (1.0 - w_last) * bconst_ref[...]
        else:
            bias = bconst_ref[...]
        update(xcat, v_group, bias, i)

    @pl.when(is_last)
    def _():
        update(knew_ref[0], lambda g: vnew_ref[0, :, g * vw:(g + 1) * vw], bnew_ref[...], npp)


def _diff_sample_kernel(pt_ref, q_ref, *refs, npp, npg, lam_init, n_new):
    k_refs = refs[:npp]
    v_refs = refs[npp:2 * npp]
    (knew_ref, vnew_ref, bconst_ref, blast_ref, bnew_ref, lamp_ref, sub_ref,
     o_ref, m_ref, l_ref, acc_ref) = refs[2 * npp:]
    p = pl.program_id(1)
    comps = 2 * A_HEADS
    v_ids = [(h, A_HEADS + h) for h in range(A_HEADS)]
    _sample_attn_body(p, npg, npp, q_ref[0], k_refs, v_refs, comps, comps, v_ids, knew_ref, vnew_ref,
                      bconst_ref, blast_ref, bnew_ref, m_ref, l_ref, acc_ref, lambda i, s: s)

    @pl.when(p == npg - 1)
    def _():
        lam = _diff_lambda(lamp_ref[...], lam_init)
        o = acc_ref[...] / l_ref[...]
        hw = 2 * A_HD
        for h in range(A_HEADS):
            r = 2 * h * SUBLANES
            od = o[r:r + SUBLANES] - lam * o[r + SUBLANES:r + 2 * SUBLANES]
            od = _head_rms(od, sub_ref[...]) * (1.0 - lam_init)
            o_ref[0, :, h * hw:(h + 1) * hw] = od[:n_new]


def _diff_attn_sample(qbd, cache_k, cache_v, pt_flat, knew, vnew, bconst, blast, bnew, lam_p, subln,
                      *, layer, n_pages, lam_init, n_new):
    db = qbd.shape[0]
    npp = SAMPLE_PAGES_PER_STEP if n_pages % SAMPLE_PAGES_PER_STEP == 0 else 1
    npg = n_pages // npp
    hw = 2 * A_HD
    comps = 2 * A_HEADS
    per_b = lambda shape: pl.BlockSpec((1,) + shape, lambda b, p, pt: (b, 0, 0))
    const = lambda shape: pl.BlockSpec(shape, lambda b, p, pt: (0, 0))
    grid_spec = pltpu.PrefetchScalarGridSpec(
        num_scalar_prefetch=1,
        grid=(db, npg),
        in_specs=([per_b((LANES, A_QK))] + _page_specs(comps, layer, n_pages, npp)
                  + _page_specs(comps, layer, n_pages, npp)
                  + [per_b((PAGE_SIZE, A_QK)), per_b((PAGE_SIZE, A_QK)),
                     const((LANES, 1)), const((LANES, PAGE_SIZE)), const((LANES, PAGE_SIZE)),
                     const((4, A_HD)), const((1, hw))]),
        out_specs=pl.BlockSpec((1, n_new, A_QK), lambda b, p, pt: (b, 0, 0)),
        scratch_shapes=[pltpu.VMEM((LANES, 1), F32), pltpu.VMEM((LANES, 1), F32),
                        pltpu.VMEM((LANES, hw), F32)],
    )
    return pl.pallas_call(
        functools.partial(_diff_sample_kernel, npp=npp, npg=npg, lam_init=lam_init, n_new=n_new),
        grid_spec=grid_spec,
        out_shape=jax.ShapeDtypeStruct((db, n_new, A_QK), F32),
        compiler_params=_cparams(("parallel", "arbitrary")),
        name="diff_attn_sample",
    )(pt_flat, qbd, *([cache_k] * npp), *([cache_v] * npp), knew, vnew, bconst, blast, bnew,
      lam_p.astype(F32), subln.reshape(1, hw).astype(F32))


def _idx_sample_kernel(pt_ref, iq_ref, w_ref, *refs, npp, npg, n_pages, n_sel, n_new):
    ik_refs = refs[:npp]
    iknew_ref, keys_ref, thr_ref = refs[npp:]
    p = pl.program_id(1)
    iq = iq_ref[0]
    w = w_ref[0]

    def slot_scores(ik):
        d = lax.dot_general(iq, ik, _NT, preferred_element_type=F32)
        r = jnp.maximum(d, 0.0) * w
        return jnp.sum(r.reshape(SUBLANES, IDX_HEADS, PAGE_SIZE), axis=1)

    for i in range(npp):
        sc = slot_scores(ik_refs[i][0, 0].astype(BF16))
        off = pl.multiple_of((p * npp + i) * PAGE_SIZE, PAGE_SIZE)
        keys_ref[0, :, pl.ds(off, PAGE_SIZE)] = _sortable_key(sc)

    @pl.when(p == npg - 1)
    def _():
        sc = slot_scores(iknew_ref[0])
        srow = lax.broadcasted_iota(I32, sc.shape, 0)
        tcol = lax.broadcasted_iota(I32, sc.shape, 1)
        sc = jnp.where((tcol <= srow) & (tcol < n_new), sc, -jnp.inf)
        keys_ref[0, :, n_pages * PAGE_SIZE:(n_pages + 1) * PAGE_SIZE] = _sortable_key(sc)

        def count_ge(cand):
            return jnp.sum(jnp.where(keys_ref[0] >= cand, 1, 0), axis=1, keepdims=True)

        t = _kth_largest_key(count_ge, n_sel, (SUBLANES, 1))
        thr_ref[0] = jnp.broadcast_to(t, (SUBLANES, LANES))


def _idx_sample(iqs, ws, cache_ki, pt_flat, iknew, *, layer, n_pages, n_sel, n_new):
    db = iqs.shape[0]
    npp = SAMPLE_PAGES_PER_STEP if n_pages % SAMPLE_PAGES_PER_STEP == 0 else 1
    npg = n_pages // npp
    width = (n_pages + 1) * PAGE_SIZE
    per_b = lambda shape: pl.BlockSpec((1,) + shape, lambda b, p, pt: (b, 0, 0))
    grid_spec = pltpu.PrefetchScalarGridSpec(
        num_scalar_prefetch=1,
        grid=(db, npg),
        in_specs=([per_b((SUBLANES * IDX_HEADS, IDX_HD)), per_b((SUBLANES * IDX_HEADS, 1))]
                  + _page_specs(1, layer, n_pages, npp) + [per_b((PAGE_SIZE, IDX_HD))]),
        out_specs=[per_b((SUBLANES, width)), per_b((SUBLANES, LANES))],
    )
    return pl.pallas_call(
        functools.partial(_idx_sample_kernel, npp=npp, npg=npg, n_pages=n_pages, n_sel=n_sel, n_new=n_new),
        grid_spec=grid_spec,
        out_shape=[jax.ShapeDtypeStruct((db, SUBLANES, width), I32),
                   jax.ShapeDtypeStruct((db, SUBLANES, LANES), I32)],
        compiler_params=_cparams(("parallel", "arbitrary")),
        name="dsa_index_sample",
    )(pt_flat, iqs, ws, *([cache_ki] * npp), iknew)


def _dsa_sample_kernel(pt_ref, q_ref, *refs, npp, npg, n_new):
    k_refs = refs[:npp]
    v_refs = refs[npp:2 * npp]
    key_refs = refs[2 * npp:3 * npp + 1]
    (thr_ref, knew_ref, vnew_ref, bconst_ref, blast_ref, bnew_ref,
     o_ref, m_ref, l_ref, acc_ref) = refs[3 * npp + 1:]
    p = pl.program_id(1)

    def mask_of(i, s):
        keys = jnp.concatenate([key_refs[i][0]] * B_HEADS, axis=0)
        thr = jnp.concatenate([thr_ref[0]] * B_HEADS, axis=0)
        return jnp.where(keys >= thr, s, NEG)

    v_ids = [(n,) for n in range(B_KV_HEADS)]
    _sample_attn_body(p, npg, npp, q_ref[0], k_refs, v_refs, B_KV_HEADS, B_KV_HEADS, v_ids, knew_ref,
                      vnew_ref, bconst_ref, blast_ref, bnew_ref, m_ref, l_ref, acc_ref, mask_of)

    @pl.when(p == npg - 1)
    def _():
        o = acc_ref[...] / l_ref[...]
        for hd in range(B_HEADS):
            o_ref[0, :, hd * B_HD:(hd + 1) * B_HD] = o[hd * SUBLANES:hd * SUBLANES + n_new]


def _dsa_attn_sample(qbd, cache_k, cache_v, keys, thr, pt_flat, knew, vnew, bconst, blast, bnew,
                     *, layer, n_pages, n_new):
    db = qbd.shape[0]
    npp = SAMPLE_PAGES_PER_STEP if n_pages % SAMPLE_PAGES_PER_STEP == 0 else 1
    npg = n_pages // npp
    per_b = lambda shape: pl.BlockSpec((1,) + shape, lambda b, p, pt: (b, 0, 0))
    const = lambda shape: pl.BlockSpec(shape, lambda b, p, pt: (0, 0))
    key_specs = [pl.BlockSpec((1, SUBLANES, PAGE_SIZE), lambda b, p, pt, i=i: (b, 0, p * npp + i))
                 for i in range(npp)]
    key_specs.append(pl.BlockSpec((1, SUBLANES, PAGE_SIZE), lambda b, p, pt: (b, 0, n_pages)))
    grid_spec = pltpu.PrefetchScalarGridSpec(
        num_scalar_prefetch=1,
        grid=(db, npg),
        in_specs=([per_b((LANES, B_KV))] + _page_specs(B_KV_HEADS, layer, n_pages, npp)
                  + _page_specs(B_KV_HEADS, layer, n_pages, npp) + key_specs
                  + [per_b((SUBLANES, LANES)), per_b((PAGE_SIZE, B_KV)), per_b((PAGE_SIZE, B_KV)),
                     const((LANES, 1)), const((LANES, PAGE_SIZE)), const((LANES, PAGE_SIZE))]),
        out_specs=pl.BlockSpec((1, n_new, B_Q), lambda b, p, pt: (b, 0, 0)),
        scratch_shapes=[pltpu.VMEM((LANES, 1), F32), pltpu.VMEM((LANES, 1), F32),
                        pltpu.VMEM((LANES, B_HD), F32)],
    )
    return pl.pallas_call(
        functools.partial(_dsa_sample_kernel, npp=npp, npg=npg, n_new=n_new),
        grid_spec=grid_spec,
        out_shape=jax.ShapeDtypeStruct((db, n_new, B_Q), F32),
        compiler_params=_cparams(("parallel", "arbitrary")),
        name="dsa_attn_sample",
    )(pt_flat, qbd, *([cache_k] * npp), *([cache_v] * npp), *([keys] * (npp + 1)), thr, knew, vnew,
      bconst, blast, bnew)


def _pad_rows(x, rows):
    return jnp.pad(x, ((0, 0), (0, rows - x.shape[1]), (0, 0)))


def kernel(x_prompt, x_sample, cache_a_k, cache_a_v, cache_b_k, cache_b_v, cache_b_kidx, page_table, meta_tokens, rel_bias_table, norm_mix, norm_ffn, a_w_in, a_w_out, a_q_norm, a_k_norm, a_lambda, a_subln, b_w_in, b_w_out, b_q_norm, b_k_norm, ffn_w1, ffn_w2):
    batch, seq, d_model = x_prompt.shape
    db, ds, _ = x_sample.shape
    depth = norm_mix.shape[0]
    n_pages = page_table.shape[1]
    past_len = n_pages * PAGE_SIZE
    t_real = seq + N_META
    t_pad = -(-t_real // Q_BLOCK) * Q_BLOCK
    tq = ATT_TILE
    n_s = db * ds
    r0 = batch * t_pad
    m_tot = r0 + TAIL_ROWS
    assert t_pad % tq == 0 and r0 % TAIL_ROWS == 0 and n_s <= TAIL_ROWS and ds <= SUBLANES
    n_sel_prompt = min(TOPK_MAX, seq // 4)
    n_sel_sample = min(TOPK_MAX, (past_len + ds) // 4)

    meta = jnp.broadcast_to(meta_tokens.astype(F32), (batch, N_META, d_model))
    xp = jnp.concatenate([meta, x_prompt, jnp.zeros((batch, t_pad - t_real, d_model), F32)], axis=1)
    x = jnp.concatenate([xp.reshape(r0, d_model), x_sample.reshape(n_s, d_model),
                         jnp.zeros((TAIL_ROWS - n_s, d_model), F32)], axis=0)

    ri = lax.broadcasted_iota(I32, (tq, tq), 0)
    ci = lax.broadcasted_iota(I32, (tq, tq), 1)
    r8 = lax.broadcasted_iota(I32, (SUBLANES, tq), 0)
    c8 = lax.broadcasted_iota(I32, (SUBLANES, tq), 1)
    dist = jnp.concatenate([ri - ci, tq + ri - ci, PAGE_SIZE + r8 - c8, r8 - c8], axis=0)
    bias_all = _bias_tiles(rel_bias_table, dist)
    bias_prompt = bias_all[:, :2 * tq]
    far = rel_bias_table[N_BUCKETS - 1].astype(F32)
    b_last = bias_all[:, 2 * tq:2 * tq + SUBLANES, :PAGE_SIZE].reshape(N_BIAS_HEADS * SUBLANES, PAGE_SIZE)
    b_new = bias_all[:, 2 * tq + SUBLANES:2 * tq + 2 * SUBLANES, :PAGE_SIZE].reshape(
        N_BIAS_HEADS * SUBLANES, PAGE_SIZE)
    b_const = jnp.repeat(far, SUBLANES).reshape(N_BIAS_HEADS * SUBLANES, 1)

    pt_flat = page_table.reshape(-1).astype(I32)
    n_a, n_pool = cache_a_k.shape[:2]
    n_b = cache_b_k.shape[0]
    cache_a_k2 = cache_a_k.reshape(n_a, n_pool, PAGE_SIZE * 2 * A_HEADS, A_HD)
    cache_a_v2 = cache_a_v.reshape(n_a, n_pool, PAGE_SIZE, A_HEADS, 2, A_HD).transpose(
        0, 1, 2, 4, 3, 5).reshape(n_a, n_pool, PAGE_SIZE * 2 * A_HEADS, A_HD)
    cache_b_k2 = cache_b_k.reshape(n_b, n_pool, PAGE_SIZE * B_KV_HEADS, B_HD)
    cache_b_v2 = cache_b_v.reshape(n_b, n_pool, PAGE_SIZE * B_KV_HEADS, B_HD)

    b_in_pad = -(-B_IN // 256) * 256
    eye_m = jnp.eye(2 * A_HEADS, dtype=BF16)
    eye_n = jnp.eye(B_KV_HEADS, dtype=BF16)

    def rows_of(a):
        return a[r0:r0 + n_s].reshape(db, ds, a.shape[1])

    def prompt_rows(a, inner):
        return a[:r0].reshape((batch, t_pad) + inner)[:, :t_real]

    outs = {k: [] for k in ("akp", "avp", "bkp", "bvp", "bip", "aks", "avs", "bks", "bvs", "bis")}
    for i in range(depth):
        j = i // N_MIXERS
        h = _rmsnorm(x, norm_mix[i])
        if i % N_MIXERS == 0:
            lam_init = 0.8 - 0.6 * math.exp(-0.3 * i)
            z = _matmul(h, a_w_in[j].astype(BF16))
            q, k, kb, vb = _a_post(z, a_q_norm[j], a_k_norm[j])
            v = z[:, 2 * A_QK:]
            o_p = _diff_attn_prompt(q, kb, vb, bias_prompt, far, a_lambda[j], a_subln[j],
                                    batch=batch, t_pad=t_pad, lam_init=lam_init)
            qs = _pad_rows(rows_of(q), SUBLANES).reshape(db, SUBLANES, 2 * A_HEADS, A_HD)
            qbd = (qs[:, None] * eye_m[None, :, None, :, None]).reshape(db, LANES, A_QK)
            o_s = _diff_attn_sample(qbd, cache_a_k2, cache_a_v2, pt_flat,
                                    _pad_rows(rows_of(kb), PAGE_SIZE), _pad_rows(rows_of(vb), PAGE_SIZE),
                                    b_const, b_last, b_new, a_lambda[j], a_subln[j],
                                    layer=j, n_pages=n_pages, lam_init=lam_init, n_new=ds)
            w_out = a_w_out[j]
            outs["akp"].append(prompt_rows(k, (A_HEADS, 2, A_HD)))
            outs["avp"].append(prompt_rows(v, (A_HEADS, 2 * A_HD)))
            outs["aks"].append(rows_of(k).reshape(db, ds, A_HEADS, 2, A_HD))
            outs["avs"].append(rows_of(v).reshape(db, ds, A_HEADS, 2 * A_HD))
        else:
            w_in = jnp.pad(b_w_in[j], ((0, 0), (0, b_in_pad - B_IN))).astype(BF16)
            z = _matmul(h, w_in)
            q, k, kb, vb, iq, ikb, iw = _b_post(z, b_q_norm[j], b_k_norm[j])
            v = z[:, B_Q + B_KV:B_Q + 2 * B_KV]
            ik = z[:, B_Q + 2 * B_KV + I_Q:B_Q + 2 * B_KV + I_Q + IDX_HD]
            o_p = _dsa_prompt(q, iq, iw, kb, vb, ikb, bias_prompt, far,
                              batch=batch, t_pad=t_pad, n_sel=n_sel_prompt)
            iqs = _pad_rows(rows_of(iq), SUBLANES).reshape(db, SUBLANES * IDX_HEADS, IDX_HD)
            ws = _pad_rows(rows_of(iw)[:, :, :IDX_HEADS], SUBLANES).reshape(db, SUBLANES * IDX_HEADS, 1)
            keys, thr = _idx_sample(iqs, ws, cache_b_kidx, pt_flat, _pad_rows(rows_of(ikb), PAGE_SIZE),
                                    layer=j, n_pages=n_pages, n_sel=n_sel_sample, n_new=ds)
            qs = _pad_rows(rows_of(q), SUBLANES).reshape(db, SUBLANES, B_KV_HEADS, B_GROUP, B_HD)
            qbd = (qs.transpose(0, 3, 1, 2, 4)[:, None]
                   * eye_n[None, :, None, None, :, None]).reshape(db, LANES, B_KV)
            o_s = _dsa_attn_sample(qbd, cache_b_k2, cache_b_v2, keys, thr, pt_flat,
                                   _pad_rows(rows_of(kb), PAGE_SIZE), _pad_rows(rows_of(vb), PAGE_SIZE),
                                   b_const, b_last, b_new, layer=j, n_pages=n_pages, n_new=ds)
            w_out = b_w_out[j]
            outs["bkp"].append(prompt_rows(k, (B_KV_HEADS, B_HD)))
            outs["bvp"].append(prompt_rows(v, (B_KV_HEADS, B_HD)))
            outs["bip"].append(prompt_rows(ik, (IDX_HD,)))
            outs["bks"].append(rows_of(k).reshape(db, ds, B_KV_HEADS, B_HD))
            outs["bvs"].append(rows_of(v).reshape(db, ds, B_KV_HEADS, B_HD))
            outs["bis"].append(rows_of(ik))
        width = o_p.shape[1]
        o = jnp.concatenate([o_p, o_s.reshape(n_s, width).astype(BF16),
                             jnp.zeros((TAIL_ROWS - n_s, width), BF16)], axis=0)
        x = _matmul(o, w_out.astype(BF16), mode="res", res=x)
        hf = _rmsnorm(x, norm_ffn[i])
        u = _matmul(hf, ffn_w1[i].astype(BF16), mode="relu2", out_dtype=BF16)
        x = _matmul(u, ffn_w2[i].astype(BF16), mode="res", res=x)

    y_prompt = x[:r0].reshape(batch, t_pad, d_model)[:, N_META:t_real]
    y_sample = x[r0:r0 + n_s].reshape(db, ds, d_model)
    st = lambda key: jnp.stack(outs[key])
    return (y_prompt, y_sample, st("akp"), st("avp"), st("bkp"), st("bvp"), st("bip"),
            st("aks"), st("avs"), st("bks"), st("bvs"), st("bis"))
```

```python
import functools
import math

import jax
import jax.numpy as jnp
from jax import lax
from jax.experimental import pallas as pl
from jax.experimental.pallas import tpu as pltpu

F32 = jnp.float32
BF16 = jnp.bfloat16
I32 = jnp.int32

N_META = 16
N_MIXERS = 2
A_HEADS = 8
A_HD = 128
A_QK = A_HEADS * 2 * A_HD
B_HEADS = 16
B_KV_HEADS = 4
B_GROUP = B_HEADS // B_KV_HEADS
B_HD = 128
B_Q = B_HEADS * B_HD
B_KV = B_KV_HEADS * B_HD
IDX_HEADS = 16
IDX_HD = 128
I_Q = IDX_HEADS * IDX_HD
B_IN = B_Q + 2 * B_KV + I_Q + IDX_HD + IDX_HEADS
TOPK_MAX = 256
N_BUCKETS = 32
MAX_DISTANCE = 128
N_BIAS_HEADS = 16
PAGE_SIZE = 128
Q_BLOCK = 128
EPS = 1e-6

LANES = 128
SUBLANES = 8
ATT_TILE = 384
TAIL_ROWS = 256
NEG = -1e30
INT_MIN = -2 ** 31
LOG2E = math.log2(math.e)
DIFF_SAMPLE_PAGES = 4
DSA_SAMPLE_PAGES = 8
IDX_SAMPLE_PAGES = 16
DSA_HEADS_PER_PASS = 2
VMEM_LIMIT_BYTES = 48 * 1024 * 1024

_NT = (((1,), (1,)), ((), ()))


def _cparams(sem):
    return pltpu.CompilerParams(dimension_semantics=sem, vmem_limit_bytes=VMEM_LIMIT_BYTES)


def _pages_per_step(n_pages, want):
    return max(c for c in range(1, want + 1) if n_pages % c == 0)


def _pick(n, candidates):
    for c in candidates:
        if n % c == 0:
            return c
    raise ValueError(f"no tile for {n} in {candidates}")


def _rms_kernel(x_ref, g_ref, o_ref):
    x = x_ref[...]
    ms = jnp.mean(x * x, axis=-1, keepdims=True)
    o_ref[...] = (x * lax.rsqrt(ms + EPS) * g_ref[...]).astype(o_ref.dtype)


def _rmsnorm(x, g):
    m, d = x.shape
    tm = _pick(m, (512, 256))
    return pl.pallas_call(
        _rms_kernel,
        grid=(m // tm,),
        in_specs=[pl.BlockSpec((tm, d), lambda i: (i, 0)),
                  pl.BlockSpec((1, d), lambda i: (0, 0))],
        out_specs=pl.BlockSpec((tm, d), lambda i: (i, 0)),
        out_shape=jax.ShapeDtypeStruct((m, d), BF16),
        compiler_params=_cparams(("parallel",)),
        name="rmsnorm",
    )(x, g.reshape(1, d).astype(F32))


def _mm_kernel(*refs, nk, mode):
    if mode == "res":
        a_ref, w_ref, r_ref, o_ref, acc_ref = refs
    else:
        a_ref, w_ref, o_ref, acc_ref = refs
        r_ref = None
    k = pl.program_id(2)

    @pl.when(k == 0)
    def _():
        acc_ref[...] = jnp.zeros_like(acc_ref)

    acc_ref[...] += jnp.dot(a_ref[...], w_ref[...], preferred_element_type=F32)

    @pl.when(k == nk - 1)
    def _():
        acc = acc_ref[...]
        if mode == "relu2":
            r = jnp.maximum(acc, 0.0)
            acc = r * r
        elif mode == "res":
            acc = acc + r_ref[...]
        o_ref[...] = acc.astype(o_ref.dtype)


def _matmul(a, w, *, mode="plain", res=None, out_dtype=F32):
    m, kd = a.shape
    n = w.shape[1]
    tm = _pick(m, (1088, 512, 256))
    tn = _pick(n, (1024, 768, 512, 256, 128))
    tk = _pick(kd, (2048, 1024, 512, 256))
    nk = kd // tk
    in_specs = [pl.BlockSpec((tm, tk), lambda i, j, k: (i, k)),
                pl.BlockSpec((tk, tn), lambda i, j, k: (k, j))]
    args = [a, w]
    if mode == "res":
        in_specs.append(pl.BlockSpec((tm, tn), lambda i, j, k: (i, j)))
        args.append(res)
    return pl.pallas_call(
        functools.partial(_mm_kernel, nk=nk, mode=mode),
        grid=(m // tm, n // tn, nk),
        in_specs=in_specs,
        out_specs=pl.BlockSpec((tm, tn), lambda i, j, k: (i, j)),
        out_shape=jax.ShapeDtypeStruct((m, n), out_dtype),
        scratch_shapes=[pltpu.VMEM((tm, tn), F32)],
        compiler_params=_cparams(("parallel", "parallel", "arbitrary")),
        name="matmul_" + mode,
    )(*args)


def _bias_kernel(tab_ref, d_ref, o_ref):
    h = pl.program_id(0)
    d = d_ref[...]
    max_exact = N_BUCKETS // 2
    n = jnp.maximum(d, 0)
    nf = jnp.maximum(n, 1).astype(F32)
    large = max_exact + (jnp.log(nf / max_exact) / math.log(MAX_DISTANCE / max_exact)
                         * (N_BUCKETS - max_exact)).astype(I32)
    bucket = jnp.where(n < max_exact, n, jnp.minimum(large, N_BUCKETS - 1))
    acc = jnp.zeros(d.shape, F32)
    for b in range(N_BUCKETS):
        acc = jnp.where(bucket == b, tab_ref[b * N_BIAS_HEADS + h], acc)
    far = tab_ref[(N_BUCKETS - 1) * N_BIAS_HEADS + h]
    o_ref[0] = jnp.where(d < 0, NEG, (acc - far) * LOG2E)


def _bias_tiles(table, dist):
    r, c = dist.shape
    return pl.pallas_call(
        _bias_kernel,
        grid=(N_BIAS_HEADS,),
        in_specs=[pl.BlockSpec(memory_space=pltpu.SMEM),
                  pl.BlockSpec((r, c), lambda h: (0, 0))],
        out_specs=pl.BlockSpec((1, r, c), lambda h: (h, 0, 0)),
        out_shape=jax.ShapeDtypeStruct((N_BIAS_HEADS, r, c), F32),
        compiler_params=_cparams(("arbitrary",)),
        name="bias_tiles",
    )(table.astype(F32).reshape(-1), dist)


def _head_rms(x, g):
    ms = jnp.mean(x * x, axis=-1, keepdims=True)
    return x * lax.rsqrt(ms + EPS) * g


def _a_post_kernel(zq_ref, zk_ref, zv_ref, qg_ref, kg_ref, q_ref, k_ref, kb_ref, vb_ref):
    scale = A_HD ** -0.5 * LOG2E
    for g in range(2 * A_HEADS):
        sl = slice(g * A_HD, (g + 1) * A_HD)
        q_ref[:, sl] = (_head_rms(zq_ref[:, sl], qg_ref[...]) * scale).astype(BF16)
        kn = _head_rms(zk_ref[:, sl], kg_ref[...])
        k_ref[:, sl] = kn
        kb_ref[:, sl] = kn.astype(BF16)
    vb_ref[...] = zv_ref[...].astype(BF16)


def _a_post(z, q_gain, k_gain):
    m = z.shape[0]
    tm = 256
    col = lambda c: pl.BlockSpec((tm, A_QK), lambda i, c=c: (i, c))
    gain = pl.BlockSpec((1, A_HD), lambda i: (0, 0))
    out = pl.BlockSpec((tm, A_QK), lambda i: (i, 0))
    return pl.pallas_call(
        _a_post_kernel,
        grid=(m // tm,),
        in_specs=[col(0), col(1), col(2), gain, gain],
        out_specs=[out, out, out, out],
        out_shape=[jax.ShapeDtypeStruct((m, A_QK), BF16), jax.ShapeDtypeStruct((m, A_QK), F32),
                   jax.ShapeDtypeStruct((m, A_QK), BF16), jax.ShapeDtypeStruct((m, A_QK), BF16)],
        compiler_params=_cparams(("parallel",)),
        name="diff_post",
    )(z, z, z, q_gain.reshape(1, A_HD).astype(F32), k_gain.reshape(1, A_HD).astype(F32))


def _b_post_kernel(zq_ref, zk_ref, zv_ref, zi0_ref, zi1_ref, zik_ref, ziw_ref, qg_ref, kg_ref,
                   q_ref, k_ref, kb_ref, vb_ref, iq_ref, ikb_ref, iw_ref):
    for g in range(B_HEADS):
        sl = slice(g * B_HD, (g + 1) * B_HD)
        q_ref[:, sl] = (_head_rms(zq_ref[:, sl], qg_ref[...]) * (B_HD ** -0.5 * LOG2E)).astype(BF16)
    for g in range(B_KV_HEADS):
        sl = slice(g * B_HD, (g + 1) * B_HD)
        kn = _head_rms(zk_ref[:, sl], kg_ref[...])
        k_ref[:, sl] = kn
        kb_ref[:, sl] = kn.astype(BF16)
    vb_ref[...] = zv_ref[...].astype(BF16)
    half = I_Q // 2
    iq_ref[:, :half] = (zi0_ref[...] * (IDX_HD ** -0.5)).astype(BF16)
    iq_ref[:, half:] = (zi1_ref[...] * (IDX_HD ** -0.5)).astype(BF16)
    ikb_ref[...] = zik_ref[...].astype(BF16)
    iw_ref[...] = ziw_ref[...] * (IDX_HEADS ** -0.5)


def _b_post(z, q_gain, k_gain):
    m = z.shape[0]
    tm = 256
    blk = lambda w, c: pl.BlockSpec((tm, w), lambda i, c=c: (i, c))
    gain = pl.BlockSpec((1, B_HD), lambda i: (0, 0))
    o1 = B_Q
    o2 = o1 + B_KV
    o3 = o2 + B_KV
    o4 = o3 + I_Q
    half = I_Q // 2
    out = lambda w: pl.BlockSpec((tm, w), lambda i: (i, 0))
    sds = lambda w, dt: jax.ShapeDtypeStruct((m, w), dt)
    return pl.pallas_call(
        _b_post_kernel,
        grid=(m // tm,),
        in_specs=[blk(B_Q, 0), blk(B_KV, o1 // B_KV), blk(B_KV, o2 // B_KV),
                  blk(half, o3 // half), blk(half, o3 // half + 1),
                  blk(IDX_HD, o4 // IDX_HD), blk(LANES, o4 // IDX_HD + 1), gain, gain],
        out_specs=[out(B_Q), out(B_KV), out(B_KV), out(B_KV), out(I_Q), out(IDX_HD), out(LANES)],
        out_shape=[sds(B_Q, BF16), sds(B_KV, F32), sds(B_KV, BF16), sds(B_KV, BF16),
                   sds(I_Q, BF16), sds(IDX_HD, BF16), sds(LANES, F32)],
        compiler_params=_cparams(("parallel",)),
        name="dsa_post",
    )(z, z, z, z, z, z, z, q_gain.reshape(1, B_HD).astype(F32), k_gain.reshape(1, B_HD).astype(F32))


def _lane_fold(x, op):
    r = x[:, :LANES]
    for c in range(1, x.shape[1] // LANES):
        r = op(r, x[:, c * LANES:(c + 1) * LANES])
    return r


def _attend(streams, v_of, s_ref, mx_ref, ls_ref, acc_ref, qi, tq):
    mx_ref[...] = jnp.full(mx_ref.shape, NEG, F32)

    def score(j, delta):
        off = pl.multiple_of(j * tq, tq)
        r0 = 0
        for q, k_of, near_bias, far_bias in streams:
            rows = slice(r0, r0 + q.shape[0])
            r0 += q.shape[0]
            s = lax.dot_general(q, k_of(off), _NT, preferred_element_type=F32)
            b = far_bias(off) if delta is None else near_bias(delta, off)
            if b is not None:
                s = s + b
            s_ref[rows, pl.ds(off, tq)] = s
            mx_ref[rows, :] = jnp.maximum(mx_ref[rows, :], _lane_fold(s, jnp.maximum))

    def far_body(j, c):
        score(j, None)
        return c

    lax.fori_loop(0, jnp.maximum(qi - 1, 0), far_body, 0)

    @pl.when(qi >= 1)
    def _():
        score(qi - 1, 1)

    score(qi, 0)

    m = jnp.max(mx_ref[...], axis=1, keepdims=True)
    mx_ref[...] = jnp.broadcast_to(m, mx_ref.shape)
    ls_ref[...] = jnp.zeros_like(ls_ref)
    acc_ref[...] = jnp.zeros_like(acc_ref)

    def pv_body(j, c):
        off = pl.multiple_of(j * tq, tq)
        s = s_ref[:, pl.ds(off, tq)]
        m_rep = mx_ref[...]
        parts = [jnp.exp2(s[:, cc * LANES:(cc + 1) * LANES] - m_rep) for cc in range(tq // LANES)]
        lsum = parts[0]
        for part in parts[1:]:
            lsum = lsum + part
        ls_ref[...] += lsum
        p = jnp.concatenate(parts, axis=1).astype(BF16)
        acc_ref[...] += jnp.dot(p, v_of(off), preferred_element_type=F32)
        return c

    lax.fori_loop(0, qi + 1, pv_body, 0)
    return acc_ref[...] / jnp.sum(ls_ref[...], axis=1, keepdims=True)


def _diff_lambda(lp, lam_init):
    s1 = jnp.sum(lp[0:1] * lp[1:2], axis=1, keepdims=True)
    s2 = jnp.sum(lp[2:3] * lp[3:4], axis=1, keepdims=True)
    return jnp.exp(s1) - jnp.exp(s2) + lam_init


def _diff_attn_kernel(q_ref, k_ref, v_ref, bias_ref, lamp_ref, sub_ref, o_ref,
                      s_ref, mx_ref, ls_ref, acc_ref, *, lam_init, tq):
    qi = pl.program_id(2)
    streams = []
    for mp in range(2):
        sl = slice(mp * A_HD, (mp + 1) * A_HD)
        streams.append((q_ref[:, sl],
                        lambda off, sl=sl: k_ref[pl.ds(off, tq), sl],
                        lambda delta, off, mp=mp: bias_ref[mp, delta * tq:(delta + 1) * tq, :],
                        lambda off: None))
    o2 = _attend(streams, lambda off: v_ref[pl.ds(off, tq), :], s_ref, mx_ref, ls_ref, acc_ref, qi, tq)
    lam = _diff_lambda(lamp_ref[...], lam_init)
    o = o2[:tq] - lam * o2[tq:]
    o_ref[...] = (_head_rms(o, sub_ref[...]) * (1.0 - lam_init)).astype(o_ref.dtype)


def _diff_attn_prompt(q, kb, vb, bias, lam_p, subln, *, batch, t_pad, lam_init):
    tq = ATT_TILE
    nq = t_pad // tq
    hw = 2 * A_HD
    return pl.pallas_call(
        functools.partial(_diff_attn_kernel, lam_init=lam_init, tq=tq),
        grid=(batch, A_HEADS, nq),
        in_specs=[pl.BlockSpec((tq, hw), lambda b, h, i: (b * nq + i, h)),
                  pl.BlockSpec((t_pad, hw), lambda b, h, i: (b, h)),
                  pl.BlockSpec((t_pad, hw), lambda b, h, i: (b, h)),
                  pl.BlockSpec((2, 2 * tq, tq), lambda b, h, i: (h, 0, 0)),
                  pl.BlockSpec((4, A_HD), lambda b, h, i: (0, 0)),
                  pl.BlockSpec((1, hw), lambda b, h, i: (0, 0))],
        out_specs=pl.BlockSpec((tq, hw), lambda b, h, i: (b * nq + i, h)),
        out_shape=jax.ShapeDtypeStruct((batch * t_pad, A_QK), BF16),
        scratch_shapes=[pltpu.VMEM((2 * tq, t_pad), F32), pltpu.VMEM((2 * tq, LANES), F32),
                        pltpu.VMEM((2 * tq, LANES), F32), pltpu.VMEM((2 * tq, hw), F32)],
        compiler_params=_cparams(("parallel", "parallel", "arbitrary")),
        name="diff_attn_prompt",
    )(q, kb, vb, bias, lam_p.astype(F32), subln.reshape(1, hw).astype(F32))


def _sortable_key(x):
    bits = lax.bitcast_convert_type(x, I32)
    return bits ^ ((bits >> 31) & 0x7FFFFFFF)


def _kth_largest_key(count_ge, n_sel, shape):
    zero = jnp.zeros(shape, I32)
    t = jnp.where(count_ge(zero) >= n_sel, zero, jnp.full(shape, INT_MIN, I32))

    def bit_body(i, t):
        cand = t + jnp.left_shift(jnp.int32(1), 30 - i)
        return jnp.where(count_ge(cand) >= n_sel, cand, t)

    return lax.fori_loop(0, 31, bit_body, t)


def _dsa_kernel(q_ref, iq_ref, iw_ref, k_ref, v_ref, ik_ref, bias_ref, o_ref,
                s_ref, mb_ref, mx_ref, ls_ref, acc_ref, *, tq, n_sel):
    qi = pl.program_id(1)
    n = pl.program_id(2)

    def keys_at(off):
        return lax.bitcast_convert_type(s_ref[0:tq, pl.ds(off, tq)], I32)

    @pl.when(n == 0)
    def _():
        row = lax.broadcasted_iota(I32, (tq, tq), 0)
        colm = lax.broadcasted_iota(I32, (tq, tq), 1)

        def score_body(j, c):
            off = pl.multiple_of(j * tq, tq)
            ikb = ik_ref[pl.ds(off, tq), :]
            acc = jnp.zeros((tq, tq), F32)
            for h in range(IDX_HEADS):
                d = lax.dot_general(iq_ref[:, h * IDX_HD:(h + 1) * IDX_HD], ikb, _NT,
                                    preferred_element_type=F32)
                acc = acc + jnp.maximum(d, 0.0) * iw_ref[:, h:h + 1]
            lim = jnp.where(j < qi, tq, 0)
            acc = jnp.where(colm <= row + lim, acc, -jnp.inf)
            s_ref[0:tq, pl.ds(off, tq)] = lax.bitcast_convert_type(_sortable_key(acc), F32)
            return c

        lax.fori_loop(0, qi + 1, score_body, 0)

        def count_ge(cand):
            def col_body(j, pc):
                ind = jnp.where(keys_at(pl.multiple_of(j * tq, tq)) >= cand, 1, 0)
                return pc + _lane_fold(ind, jnp.add)
            pc = lax.fori_loop(0, qi + 1, col_body, jnp.zeros((tq, LANES), I32))
            return jnp.sum(pc, axis=1, keepdims=True)

        thr = _kth_largest_key(count_ge, n_sel, (tq, 1))

        def mask_body(j, c):
            off = pl.multiple_of(j * tq, tq)
            mb_ref[:, pl.ds(off, tq)] = jnp.where(keys_at(off) >= thr, 0.0, NEG)
            return c

        lax.fori_loop(0, qi + 1, mask_body, 0)

    hp = DSA_HEADS_PER_PASS
    for g0 in range(0, B_GROUP, hp):
        gs = range(g0, g0 + hp)
        q = jnp.concatenate([q_ref[:, g * B_HD:(g + 1) * B_HD] for g in gs], axis=0)

        def far_bias(off):
            mb = mb_ref[:, pl.ds(off, tq)]
            return jnp.concatenate([mb] * hp, axis=0)

        def near_bias(delta, off, gs=gs):
            mb = mb_ref[:, pl.ds(off, tq)]
            return jnp.concatenate([mb + bias_ref[g, delta * tq:(delta + 1) * tq, :] for g in gs], axis=0)

        o = _attend([(q, lambda off: k_ref[pl.ds(off, tq), :], near_bias, far_bias)],
                    lambda off: v_ref[pl.ds(off, tq), :], s_ref, mx_ref, ls_ref, acc_ref, qi, tq)
        for i, g in enumerate(gs):
            o_ref[:, g * B_HD:(g + 1) * B_HD] = o[i * tq:(i + 1) * tq].astype(o_ref.dtype)


def _dsa_prompt(q, iq, iw, kb, vb, ikb, bias, *, batch, t_pad, n_sel):
    tq = ATT_TILE
    hp = DSA_HEADS_PER_PASS
    nq = t_pad // tq
    gw = B_GROUP * B_HD
    return pl.pallas_call(
        functools.partial(_dsa_kernel, tq=tq, n_sel=n_sel),
        grid=(batch, nq, B_KV_HEADS),
        in_specs=[pl.BlockSpec((tq, gw), lambda b, i, n: (b * nq + i, n)),
                  pl.BlockSpec((tq, I_Q), lambda b, i, n: (b * nq + i, 0)),
                  pl.BlockSpec((tq, LANES), lambda b, i, n: (b * nq + i, 0)),
                  pl.BlockSpec((t_pad, B_HD), lambda b, i, n: (b, n)),
                  pl.BlockSpec((t_pad, B_HD), lambda b, i, n: (b, n)),
                  pl.BlockSpec((t_pad, IDX_HD), lambda b, i, n: (b, 0)),
                  pl.BlockSpec((B_GROUP, 2 * tq, tq), lambda b, i, n: (n, 0, 0))],
        out_specs=pl.BlockSpec((tq, gw), lambda b, i, n: (b * nq + i, n)),
        out_shape=jax.ShapeDtypeStruct((batch * t_pad, B_Q), BF16),
        scratch_shapes=[pltpu.VMEM((hp * tq, t_pad), F32), pltpu.VMEM((tq, t_pad), F32),
                        pltpu.VMEM((hp * tq, LANES), F32), pltpu.VMEM((hp * tq, LANES), F32),
                        pltpu.VMEM((hp * tq, B_HD), F32)],
        compiler_params=_cparams(("parallel", "arbitrary", "arbitrary")),
        name="dsa_prompt",
    )(q, iq, iw, kb, vb, ikb, bias)


def _page_specs(comps, layer, n_pages, npp):
    def spec(i):
        return pl.BlockSpec((1, 1, PAGE_SIZE * comps, LANES),
                            lambda b, p, pt, i=i: (layer, pt[b * n_pages + p * npp + i], 0, 0))
    return [spec(i) for i in range(npp)]


def _component(ref, c, comps):
    return ref[0, 0, pl.ds(c, PAGE_SIZE, stride=comps), :].astype(BF16)


def _sample_attn_body(p, npg, npp, qbd, k_refs, v_refs, kcomps, vcomps, v_ids, knew_ref, vnew_ref,
                      blast_ref, bnew_ref, m_ref, l_ref, acc_ref, mask_of):
    groups = len(v_ids)
    rg = LANES // groups
    vw = acc_ref.shape[1]

    @pl.when(p == 0)
    def _():
        m_ref[...] = jnp.full(m_ref.shape, NEG, F32)
        l_ref[...] = jnp.zeros_like(l_ref)
        acc_ref[...] = jnp.zeros_like(acc_ref)

    def scores(xcat, bias, mask_idx):
        s = lax.dot_general(qbd, xcat, _NT, preferred_element_type=F32)
        if bias is not None:
            s = s + bias
        return mask_of(mask_idx, s)

    def update(s_blocks, v_groups):
        s = jnp.concatenate(s_blocks, axis=1) if len(s_blocks) > 1 else s_blocks[0]
        m_old = m_ref[...]
        m_new = jnp.maximum(m_old, jnp.max(s, axis=1, keepdims=True))
        alpha = jnp.exp2(m_old - m_new)
        pr = jnp.exp2(s - m_new)
        l_ref[...] = alpha * l_ref[...] + jnp.sum(pr, axis=1, keepdims=True)
        m_ref[...] = m_new
        pb = pr.astype(BF16)
        for g in range(groups):
            rows = slice(g * rg, (g + 1) * rg)
            pv = None
            for i, v_group in enumerate(v_groups):
                d = jnp.dot(pb[rows, i * PAGE_SIZE:(i + 1) * PAGE_SIZE], v_group(g),
                            preferred_element_type=F32)
                pv = d if pv is None else pv + d
            acc_ref[rows, :] = alpha[rows] * acc_ref[rows, :] + pv

    is_last = p == npg - 1
    s_blocks, v_groups = [], []
    for i in range(npp):
        xcat = jnp.concatenate([_component(k_refs[i], c, kcomps) for c in range(kcomps)], axis=1)

        def v_group(g, i=i):
            parts = [_component(v_refs[i], c, vcomps) for c in v_ids[g]]
            return parts[0] if len(parts) == 1 else jnp.concatenate(parts, axis=1)

        bias = jnp.where(is_last, 1.0, 0.0) * blast_ref[...] if i == npp - 1 else None
        s_blocks.append(scores(xcat, bias, i))
        v_groups.append(v_group)
    update(s_blocks, v_groups)

    @pl.when(is_last)
    def _():
        update([scores(knew_ref[0], bnew_ref[...], npp)],
               [lambda g: vnew_ref[0, :, g * vw:(g + 1) * vw]])


def _diff_sample_kernel(pt_ref, q_ref, *refs, npp, npg, lam_init, n_new):
    k_refs = refs[:npp]
    v_refs = refs[npp:2 * npp]
    (knew_ref, vnew_ref, blast_ref, bnew_ref, lamp_ref, sub_ref,
     o_ref, m_ref, l_ref, acc_ref) = refs[2 * npp:]
    p = pl.program_id(1)
    comps = 2 * A_HEADS
    v_ids = [(h, A_HEADS + h) for h in range(A_HEADS)]
    _sample_attn_body(p, npg, npp, q_ref[0], k_refs, v_refs, comps, comps, v_ids, knew_ref, vnew_ref,
                      blast_ref, bnew_ref, m_ref, l_ref, acc_ref, lambda i, s: s)

    @pl.when(p == npg - 1)
    def _():
        lam = _diff_lambda(lamp_ref[...], lam_init)
        o = acc_ref[...] / l_ref[...]
        hw = 2 * A_HD
        for h in range(A_HEADS):
            r = 2 * h * SUBLANES
            od = o[r:r + SUBLANES] - lam * o[r + SUBLANES:r + 2 * SUBLANES]
            od = _head_rms(od, sub_ref[...]) * (1.0 - lam_init)
            o_ref[0, :, h * hw:(h + 1) * hw] = od[:n_new]


def _diff_attn_sample(qbd, cache_k, cache_v, pt_flat, knew, vnew, blast, bnew, lam_p, subln,
                      *, layer, n_pages, lam_init, n_new):
    db = qbd.shape[0]
    npp = _pages_per_step(n_pages, DIFF_SAMPLE_PAGES)
    npg = n_pages // npp
    hw = 2 * A_HD
    comps = 2 * A_HEADS
    per_b = lambda shape: pl.BlockSpec((1,) + shape, lambda b, p, pt: (b, 0, 0))
    const = lambda shape: pl.BlockSpec(shape, lambda b, p, pt: (0, 0))
    grid_spec = pltpu.PrefetchScalarGridSpec(
        num_scalar_prefetch=1,
        grid=(db, npg),
        in_specs=([per_b((LANES, A_QK))] + _page_specs(comps, layer, n_pages, npp)
                  + _page_specs(comps, layer, n_pages, npp)
                  + [per_b((PAGE_SIZE, A_QK)), per_b((PAGE_SIZE, A_QK)),
                     const((LANES, PAGE_SIZE)), const((LANES, PAGE_SIZE)),
                     const((4, A_HD)), const((1, hw))]),
        out_specs=pl.BlockSpec((1, n_new, A_QK), lambda b, p, pt: (b, 0, 0)),
        scratch_shapes=[pltpu.VMEM((LANES, 1), F32), pltpu.VMEM((LANES, 1), F32),
                        pltpu.VMEM((LANES, hw), F32)],
    )
    return pl.pallas_call(
        functools.partial(_diff_sample_kernel, npp=npp, npg=npg, lam_init=lam_init, n_new=n_new),
        grid_spec=grid_spec,
        out_shape=jax.ShapeDtypeStruct((db, n_new, A_QK), F32),
        compiler_params=_cparams(("parallel", "arbitrary")),
        name="diff_attn_sample",
    )(pt_flat, qbd, *([cache_k] * npp), *([cache_v] * npp), knew, vnew, blast, bnew,
      lam_p.astype(F32), subln.reshape(1, hw).astype(F32))


def _idx_sample_kernel(pt_ref, iq_ref, w_ref, *refs, npp, npg, n_pages, n_sel, n_new):
    ik_refs = refs[:npp]
    iknew_ref, keys_ref, thr_ref = refs[npp:]
    p = pl.program_id(1)
    iq = iq_ref[0]
    w = w_ref[0]

    def slot_scores(ik):
        d = lax.dot_general(iq, ik, _NT, preferred_element_type=F32)
        r = jnp.maximum(d, 0.0) * w
        return jnp.sum(r.reshape(SUBLANES, IDX_HEADS, PAGE_SIZE), axis=1)

    for i in range(npp):
        sc = slot_scores(ik_refs[i][0, 0].astype(BF16))
        off = pl.multiple_of((p * npp + i) * PAGE_SIZE, PAGE_SIZE)
        keys_ref[0, :, pl.ds(off, PAGE_SIZE)] = _sortable_key(sc)

    @pl.when(p == npg - 1)
    def _():
        sc = slot_scores(iknew_ref[0])
        srow = lax.broadcasted_iota(I32, sc.shape, 0)
        tcol = lax.broadcasted_iota(I32, sc.shape, 1)
        sc = jnp.where((tcol <= srow) & (tcol < n_new), sc, -jnp.inf)
        keys_ref[0, :, n_pages * PAGE_SIZE:(n_pages + 1) * PAGE_SIZE] = _sortable_key(sc)

        def count_ge(cand):
            return jnp.sum(jnp.where(keys_ref[0] >= cand, 1, 0), axis=1, keepdims=True)

        t = _kth_largest_key(count_ge, n_sel, (SUBLANES, 1))
        thr_ref[0] = jnp.broadcast_to(t, (SUBLANES, LANES))


def _idx_sample(iqs, ws, cache_ki, pt_flat, iknew, *, layer, n_pages, n_sel, n_new):
    db = iqs.shape[0]
    npp = _pages_per_step(n_pages, IDX_SAMPLE_PAGES)
    npg = n_pages // npp
    width = (n_pages + 1) * PAGE_SIZE
    per_b = lambda shape: pl.BlockSpec((1,) + shape, lambda b, p, pt: (b, 0, 0))
    grid_spec = pltpu.PrefetchScalarGridSpec(
        num_scalar_prefetch=1,
        grid=(db, npg),
        in_specs=([per_b((SUBLANES * IDX_HEADS, IDX_HD)), per_b((SUBLANES * IDX_HEADS, 1))]
                  + _page_specs(1, layer, n_pages, npp) + [per_b((PAGE_SIZE, IDX_HD))]),
        out_specs=[per_b((SUBLANES, width)), per_b((SUBLANES, LANES))],
    )
    return pl.pallas_call(
        functools.partial(_idx_sample_kernel, npp=npp, npg=npg, n_pages=n_pages, n_sel=n_sel, n_new=n_new),
        grid_spec=grid_spec,
        out_shape=[jax.ShapeDtypeStruct((db, SUBLANES, width), I32),
                   jax.ShapeDtypeStruct((db, SUBLANES, LANES), I32)],
        compiler_params=_cparams(("parallel", "arbitrary")),
        name="dsa_index_sample",
    )(pt_flat, iqs, ws, *([cache_ki] * npp), iknew)


def _dsa_sample_kernel(pt_ref, q_ref, *refs, npp, npg, n_new):
    k_refs = refs[:npp]
    v_refs = refs[npp:2 * npp]
    key_refs = refs[2 * npp:3 * npp + 1]
    (thr_ref, knew_ref, vnew_ref, blast_ref, bnew_ref,
     o_ref, m_ref, l_ref, acc_ref) = refs[3 * npp + 1:]
    p = pl.program_id(1)

    def mask_of(i, s):
        keys = jnp.concatenate([key_refs[i][0]] * B_HEADS, axis=0)
        thr = jnp.concatenate([thr_ref[0]] * B_HEADS, axis=0)
        return jnp.where(keys >= thr, s, NEG)

    v_ids = [(n,) for n in range(B_KV_HEADS)]
    _sample_attn_body(p, npg, npp, q_ref[0], k_refs, v_refs, B_KV_HEADS, B_KV_HEADS, v_ids, knew_ref,
                      vnew_ref, blast_ref, bnew_ref, m_ref, l_ref, acc_ref, mask_of)

    @pl.when(p == npg - 1)
    def _():
        o = acc_ref[...] / l_ref[...]
        for hd in range(B_HEADS):
            o_ref[0, :, hd * B_HD:(hd + 1) * B_HD] = o[hd * SUBLANES:hd * SUBLANES + n_new]


def _dsa_attn_sample(qbd, cache_k, cache_v, keys, thr, pt_flat, knew, vnew, blast, bnew,
                     *, layer, n_pages, n_new):
    db = qbd.shape[0]
    npp = _pages_per_step(n_pages, DSA_SAMPLE_PAGES)
    npg = n_pages // npp
    per_b = lambda shape: pl.BlockSpec((1,) + shape, lambda b, p, pt: (b, 0, 0))
    const = lambda shape: pl.BlockSpec(shape, lambda b, p, pt: (0, 0))
    key_specs = [pl.BlockSpec((1, SUBLANES, PAGE_SIZE), lambda b, p, pt, i=i: (b, 0, p * npp + i))
                 for i in range(npp)]
    key_specs.append(pl.BlockSpec((1, SUBLANES, PAGE_SIZE), lambda b, p, pt: (b, 0, n_pages)))
    grid_spec = pltpu.PrefetchScalarGridSpec(
        num_scalar_prefetch=1,
        grid=(db, npg),
        in_specs=([per_b((LANES, B_KV))] + _page_specs(B_KV_HEADS, layer, n_pages, npp)
                  + _page_specs(B_KV_HEADS, layer, n_pages, npp) + key_specs
                  + [per_b((SUBLANES, LANES)), per_b((PAGE_SIZE, B_KV)), per_b((PAGE_SIZE, B_KV)),
                     const((LANES, PAGE_SIZE)), const((LANES, PAGE_SIZE))]),
        out_specs=pl.BlockSpec((1, n_new, B_Q), lambda b, p, pt: (b, 0, 0)),
        scratch_shapes=[pltpu.VMEM((LANES, 1), F32), pltpu.VMEM((LANES, 1), F32),
                        pltpu.VMEM((LANES, B_HD), F32)],
    )
    return pl.pallas_call(
        functools.partial(_dsa_sample_kernel, npp=npp, npg=npg, n_new=n_new),
        grid_spec=grid_spec,
        out_shape=jax.ShapeDtypeStruct((db, n_new, B_Q), F32),
        compiler_params=_cparams(("parallel", "arbitrary")),
        name="dsa_attn_sample",
    )(pt_flat, qbd, *([cache_k] * npp), *([cache_v] * npp), *([keys] * (npp + 1)), thr, knew, vnew,
      blast, bnew)


def _pad_rows(x, rows):
    return jnp.pad(x, ((0, 0), (0, rows - x.shape[1]), (0, 0)))


def kernel(x_prompt, x_sample, cache_a_k, cache_a_v, cache_b_k, cache_b_v, cache_b_kidx, page_table, meta_tokens, rel_bias_table, norm_mix, norm_ffn, a_w_in, a_w_out, a_q_norm, a_k_norm, a_lambda, a_subln, b_w_in, b_w_out, b_q_norm, b_k_norm, ffn_w1, ffn_w2):
    batch, seq, d_model = x_prompt.shape
    db, ds, _ = x_sample.shape
    depth = norm_mix.shape[0]
    n_pages = page_table.shape[1]
    past_len = n_pages * PAGE_SIZE
    t_real = seq + N_META
    t_pad = -(-t_real // Q_BLOCK) * Q_BLOCK
    tq = ATT_TILE
    n_s = db * ds
    r0 = batch * t_pad
    m_tot = r0 + TAIL_ROWS
    assert t_pad % tq == 0 and r0 % TAIL_ROWS == 0 and n_s <= TAIL_ROWS and ds <= SUBLANES
    n_sel_prompt = min(TOPK_MAX, seq // 4)
    n_sel_sample = min(TOPK_MAX, (past_len + ds) // 4)

    meta = jnp.broadcast_to(meta_tokens.astype(F32), (batch, N_META, d_model))
    xp = jnp.concatenate([meta, x_prompt, jnp.zeros((batch, t_pad - t_real, d_model), F32)], axis=1)
    x = jnp.concatenate([xp.reshape(r0, d_model), x_sample.reshape(n_s, d_model),
                         jnp.zeros((TAIL_ROWS - n_s, d_model), F32)], axis=0)

    ri = lax.broadcasted_iota(I32, (tq, tq), 0)
    ci = lax.broadcasted_iota(I32, (tq, tq), 1)
    r8 = lax.broadcasted_iota(I32, (SUBLANES, tq), 0)
    c8 = lax.broadcasted_iota(I32, (SUBLANES, tq), 1)
    dist = jnp.concatenate([ri - ci, tq + ri - ci, PAGE_SIZE + r8 - c8, r8 - c8], axis=0)
    bias_all = _bias_tiles(rel_bias_table, dist)
    bias_prompt = bias_all[:, :2 * tq]
    b_last = bias_all[:, 2 * tq:2 * tq + SUBLANES, :PAGE_SIZE].reshape(N_BIAS_HEADS * SUBLANES, PAGE_SIZE)
    b_new = bias_all[:, 2 * tq + SUBLANES:2 * tq + 2 * SUBLANES, :PAGE_SIZE].reshape(
        N_BIAS_HEADS * SUBLANES, PAGE_SIZE)

    pt_flat = page_table.reshape(-1).astype(I32)
    n_a, n_pool = cache_a_k.shape[:2]
    n_b = cache_b_k.shape[0]
    cache_a_k2 = cache_a_k.reshape(n_a, n_pool, PAGE_SIZE * 2 * A_HEADS, A_HD)
    cache_a_v2 = cache_a_v.reshape(n_a, n_pool, PAGE_SIZE, A_HEADS, 2, A_HD).transpose(
        0, 1, 2, 4, 3, 5).reshape(n_a, n_pool, PAGE_SIZE * 2 * A_HEADS, A_HD)
    cache_b_k2 = cache_b_k.reshape(n_b, n_pool, PAGE_SIZE * B_KV_HEADS, B_HD)
    cache_b_v2 = cache_b_v.reshape(n_b, n_pool, PAGE_SIZE * B_KV_HEADS, B_HD)

    b_in_pad = -(-B_IN // 256) * 256
    eye_m = jnp.eye(2 * A_HEADS, dtype=BF16)
    eye_n = jnp.eye(B_KV_HEADS, dtype=BF16)

    def rows_of(a):
        return a[r0:r0 + n_s].reshape(db, ds, a.shape[1])

    def prompt_rows(a, inner):
        return a[:r0].reshape((batch, t_pad) + inner)[:, :t_real]

    outs = {k: [] for k in ("akp", "avp", "bkp", "bvp", "bip", "aks", "avs", "bks", "bvs", "bis")}
    for i in range(depth):
        j = i // N_MIXERS
        h = _rmsnorm(x, norm_mix[i])
        if i % N_MIXERS == 0:
            lam_init = 0.8 - 0.6 * math.exp(-0.3 * i)
            z = _matmul(h, a_w_in[j].astype(BF16))
            q, k, kb, vb = _a_post(z, a_q_norm[j], a_k_norm[j])
            v = z[:, 2 * A_QK:]
            o_p = _diff_attn_prompt(q, kb, vb, bias_prompt, a_lambda[j], a_subln[j],
                                    batch=batch, t_pad=t_pad, lam_init=lam_init)
            qs = _pad_rows(rows_of(q), SUBLANES).reshape(db, SUBLANES, 2 * A_HEADS, A_HD)
            qbd = (qs[:, None] * eye_m[None, :, None, :, None]).reshape(db, LANES, A_QK)
            o_s = _diff_attn_sample(qbd, cache_a_k2, cache_a_v2, pt_flat,
                                    _pad_rows(rows_of(kb), PAGE_SIZE), _pad_rows(rows_of(vb), PAGE_SIZE),
                                    b_last, b_new, a_lambda[j], a_subln[j],
                                    layer=j, n_pages=n_pages, lam_init=lam_init, n_new=ds)
            w_out = a_w_out[j]
            outs["akp"].append(prompt_rows(k, (A_HEADS, 2, A_HD)))
            outs["avp"].append(prompt_rows(v, (A_HEADS, 2 * A_HD)))
            outs["aks"].append(rows_of(k).reshape(db, ds, A_HEADS, 2, A_HD))
            outs["avs"].append(rows_of(v).reshape(db, ds, A_HEADS, 2 * A_HD))
        else:
            w_in = jnp.pad(b_w_in[j], ((0, 0), (0, b_in_pad - B_IN))).astype(BF16)
            z = _matmul(h, w_in)
            q, k, kb, vb, iq, ikb, iw = _b_post(z, b_q_norm[j], b_k_norm[j])
            v = z[:, B_Q + B_KV:B_Q + 2 * B_KV]
            ik = z[:, B_Q + 2 * B_KV + I_Q:B_Q + 2 * B_KV + I_Q + IDX_HD]
            o_p = _dsa_prompt(q, iq, iw, kb, vb, ikb, bias_prompt,
                              batch=batch, t_pad=t_pad, n_sel=n_sel_prompt)
            iqs = _pad_rows(rows_of(iq), SUBLANES).reshape(db, SUBLANES * IDX_HEADS, IDX_HD)
            ws = _pad_rows(rows_of(iw)[:, :, :IDX_HEADS], SUBLANES).reshape(db, SUBLANES * IDX_HEADS, 1)
            keys, thr = _idx_sample(iqs, ws, cache_b_kidx, pt_flat, _pad_rows(rows_of(ikb), PAGE_SIZE),
                                    layer=j, n_pages=n_pages, n_sel=n_sel_sample, n_new=ds)
            qs = _pad_rows(rows_of(q), SUBLANES).reshape(db, SUBLANES, B_KV_HEADS, B_GROUP, B_HD)
            qbd = (qs.transpose(0, 3, 1, 2, 4)[:, None]
                   * eye_n[None, :, None, None, :, None]).reshape(db, LANES, B_KV)
            o_s = _dsa_attn_sample(qbd, cache_b_k2, cache_b_v2, keys, thr, pt_flat,
                                   _pad_rows(rows_of(kb), PAGE_SIZE), _pad_rows(rows_of(vb), PAGE_SIZE),
                                   b_last, b_new, layer=j, n_pages=n_pages, n_new=ds)
            w_out = b_w_out[j]
            outs["bkp"].append(prompt_rows(k, (B_KV_HEADS, B_HD)))
            outs["bvp"].append(prompt_rows(v, (B_KV_HEADS, B_HD)))
            outs["bip"].append(prompt_rows(ik, (IDX_HD,)))
            outs["bks"].append(rows_of(k).reshape(db, ds, B_KV_HEADS, B_HD))
            outs["bvs"].append(rows_of(v).reshape(db, ds, B_KV_HEADS, B_HD))
            outs["bis"].append(rows_of(ik))
        width = o_p.shape[1]
        o = jnp.concatenate([o_p, o_s.reshape(n_s, width).astype(BF16),
                             jnp.zeros((TAIL_ROWS - n_s, width), BF16)], axis=0)
        x = _matmul(o, w_out.astype(BF16), mode="res", res=x)
        hf = _rmsnorm(x, norm_ffn[i])
        u = _matmul(hf, ffn_w1[i].astype(BF16), mode="relu2", out_dtype=BF16)
        x = _matmul(u, ffn_w2[i].astype(BF16), mode="res", res=x)

    y_prompt = x[:r0].reshape(batch, t_pad, d_model)[:, N_META:t_real]
    y_sample = x[r0:r0 + n_s].reshape(db, ds, d_model)
    st = lambda key: jnp.stack(outs[key])
    return (y_prompt, y_sample, st("akp"), st("avp"), st("bkp"), st("bvp"), st("bip"),
            st("aks"), st("avs"), st("bks"), st("bvs"), st("bis"))
```

```python
import functools
import math

import jax
import jax.numpy as jnp
from jax import lax
from jax.experimental import pallas as pl
from jax.experimental.pallas import tpu as pltpu

F32 = jnp.float32
BF16 = jnp.bfloat16
I32 = jnp.int32

N_META = 16
N_MIXERS = 2
A_HEADS = 8
A_HD = 128
A_QK = A_HEADS * 2 * A_HD
B_HEADS = 16
B_KV_HEADS = 4
B_GROUP = B_HEADS // B_KV_HEADS
B_HD = 128
B_Q = B_HEADS * B_HD
B_KV = B_KV_HEADS * B_HD
IDX_HEADS = 16
IDX_HD = 128
I_Q = IDX_HEADS * IDX_HD
B_IN = B_Q + 2 * B_KV + I_Q + IDX_HD + IDX_HEADS
TOPK_MAX = 256
N_BUCKETS = 32
MAX_DISTANCE = 128
N_BIAS_HEADS = 16
PAGE_SIZE = 128
Q_BLOCK = 128
EPS = 1e-6

LANES = 128
SUBLANES = 8
ATT_TILE = 384
TAIL_ROWS = 256
NEG = -1e30
INT_MIN = -2 ** 31
LOG2E = math.log2(math.e)
DIFF_SAMPLE_PAGES = 4
DSA_SAMPLE_PAGES = 8
IDX_SAMPLE_PAGES = 16
CHUNKS_PER_TRIP = 4
DSA_HEADS_PER_PASS = 2
VMEM_LIMIT_BYTES = 48 * 1024 * 1024
DSA_VMEM_LIMIT_BYTES = 58 * 1024 * 1024
I16 = jnp.int16

_NT = (((1,), (1,)), ((), ()))


def _cparams(sem, vmem_limit=VMEM_LIMIT_BYTES):
    return pltpu.CompilerParams(dimension_semantics=sem, vmem_limit_bytes=vmem_limit)


def _pages_per_step(n_pages, want):
    return max(c for c in range(1, want + 1) if n_pages % c == 0)


def _pick(n, candidates):
    for c in candidates:
        if n % c == 0:
            return c
    raise ValueError(f"no tile for {n} in {candidates}")


def _rms_kernel(x_ref, g_ref, o_ref):
    x = x_ref[...]
    ms = jnp.mean(x * x, axis=-1, keepdims=True)
    o_ref[...] = (x * lax.rsqrt(ms + EPS) * g_ref[...]).astype(o_ref.dtype)


def _rmsnorm(x, g):
    m, d = x.shape
    tm = _pick(m, (512, 256))
    return pl.pallas_call(
        _rms_kernel,
        grid=(m // tm,),
        in_specs=[pl.BlockSpec((tm, d), lambda i: (i, 0)),
                  pl.BlockSpec((1, d), lambda i: (0, 0))],
        out_specs=pl.BlockSpec((tm, d), lambda i: (i, 0)),
        out_shape=jax.ShapeDtypeStruct((m, d), BF16),
        compiler_params=_cparams(("parallel",)),
        name="rmsnorm",
    )(x, g.reshape(1, d).astype(F32))


def _mm_kernel(*refs, nk, mode):
    if mode == "res":
        a_ref, w_ref, r_ref, o_ref, acc_ref = refs
    else:
        a_ref, w_ref, o_ref, acc_ref = refs
        r_ref = None
    k = pl.program_id(2)

    @pl.when(k == 0)
    def _():
        acc_ref[...] = jnp.zeros_like(acc_ref)

    acc_ref[...] += jnp.dot(a_ref[...], w_ref[...], preferred_element_type=F32)

    @pl.when(k == nk - 1)
    def _():
        acc = acc_ref[...]
        if mode == "relu2":
            r = jnp.maximum(acc, 0.0)
            acc = r * r
        elif mode == "res":
            acc = acc + r_ref[...]
        o_ref[...] = acc.astype(o_ref.dtype)


def _matmul(a, w, *, mode="plain", res=None, out_dtype=F32):
    m, kd = a.shape
    n = w.shape[1]
    tm = _pick(m, (1088, 512, 256))
    tn = _pick(n, (1024, 768, 512, 256, 128))
    tk = _pick(kd, (2048, 1024, 512, 256))
    nk = kd // tk
    in_specs = [pl.BlockSpec((tm, tk), lambda i, j, k: (i, k)),
                pl.BlockSpec((tk, tn), lambda i, j, k: (k, j))]
    args = [a, w]
    if mode == "res":
        in_specs.append(pl.BlockSpec((tm, tn), lambda i, j, k: (i, j)))
        args.append(res)
    return pl.pallas_call(
        functools.partial(_mm_kernel, nk=nk, mode=mode),
        grid=(m // tm, n // tn, nk),
        in_specs=in_specs,
        out_specs=pl.BlockSpec((tm, tn), lambda i, j, k: (i, j)),
        out_shape=jax.ShapeDtypeStruct((m, n), out_dtype),
        scratch_shapes=[pltpu.VMEM((tm, tn), F32)],
        compiler_params=_cparams(("parallel", "parallel", "arbitrary")),
        name="matmul_" + mode,
    )(*args)


def _bias_kernel(tab_ref, d_ref, o_ref):
    h = pl.program_id(0)
    d = d_ref[...]
    max_exact = N_BUCKETS // 2
    n = jnp.maximum(d, 0)
    nf = jnp.maximum(n, 1).astype(F32)
    large = max_exact + (jnp.log(nf / max_exact) / math.log(MAX_DISTANCE / max_exact)
                         * (N_BUCKETS - max_exact)).astype(I32)
    bucket = jnp.where(n < max_exact, n, jnp.minimum(large, N_BUCKETS - 1))
    acc = jnp.zeros(d.shape, F32)
    for b in range(N_BUCKETS):
        acc = jnp.where(bucket == b, tab_ref[b * N_BIAS_HEADS + h], acc)
    far = tab_ref[(N_BUCKETS - 1) * N_BIAS_HEADS + h]
    o_ref[0] = jnp.where(d < 0, NEG, (acc - far) * LOG2E)


def _bias_tiles(table, dist):
    r, c = dist.shape
    return pl.pallas_call(
        _bias_kernel,
        grid=(N_BIAS_HEADS,),
        in_specs=[pl.BlockSpec(memory_space=pltpu.SMEM),
                  pl.BlockSpec((r, c), lambda h: (0, 0))],
        out_specs=pl.BlockSpec((1, r, c), lambda h: (h, 0, 0)),
        out_shape=jax.ShapeDtypeStruct((N_BIAS_HEADS, r, c), F32),
        compiler_params=_cparams(("arbitrary",)),
        name="bias_tiles",
    )(table.astype(F32).reshape(-1), dist)


def _head_rms(x, g):
    ms = jnp.mean(x * x, axis=-1, keepdims=True)
    return x * lax.rsqrt(ms + EPS) * g


def _a_post_kernel(zq_ref, zk_ref, zv_ref, qg_ref, kg_ref, q_ref, k_ref, kb_ref, vb_ref, v_ref, *, tm):
    scale = A_HD ** -0.5 * LOG2E
    comps = 2 * A_HEADS
    for g in range(comps):
        sl = slice(g * A_HD, (g + 1) * A_HD)
        q_ref[:, sl] = (_head_rms(zq_ref[:, sl], qg_ref[...]) * scale).astype(BF16)
        kn = _head_rms(zk_ref[:, sl], kg_ref[...])
        k_ref[pl.ds(g, tm, stride=comps), :] = kn
        kb_ref[:, sl] = kn.astype(BF16)
        head, half = g // 2, g % 2
        v_ref[pl.ds(half * A_HEADS + head, tm, stride=comps), :] = zv_ref[:, sl]
    vb_ref[...] = zv_ref[...].astype(BF16)


def _a_post(z, q_gain, k_gain):
    m = z.shape[0]
    tm = 256
    comps = 2 * A_HEADS
    col = lambda c: pl.BlockSpec((tm, A_QK), lambda i, c=c: (i, c))
    gain = pl.BlockSpec((1, A_HD), lambda i: (0, 0))
    out = pl.BlockSpec((tm, A_QK), lambda i: (i, 0))
    lin = pl.BlockSpec((tm * comps, A_HD), lambda i: (i, 0))
    return pl.pallas_call(
        functools.partial(_a_post_kernel, tm=tm),
        grid=(m // tm,),
        in_specs=[col(0), col(1), col(2), gain, gain],
        out_specs=[out, lin, out, out, lin],
        out_shape=[jax.ShapeDtypeStruct((m, A_QK), BF16), jax.ShapeDtypeStruct((m * comps, A_HD), F32),
                   jax.ShapeDtypeStruct((m, A_QK), BF16), jax.ShapeDtypeStruct((m, A_QK), BF16),
                   jax.ShapeDtypeStruct((m * comps, A_HD), F32)],
        compiler_params=_cparams(("parallel",)),
        name="diff_post",
    )(z, z, z, q_gain.reshape(1, A_HD).astype(F32), k_gain.reshape(1, A_HD).astype(F32))


def _b_post_kernel(zq_ref, zk_ref, zv_ref, zi0_ref, zi1_ref, zik_ref, ziw_ref, qg_ref, kg_ref,
                   q_ref, k_ref, kb_ref, vb_ref, iq_ref, ikb_ref, iw_ref, v_ref, ik_ref, *, tm):
    for g in range(B_HEADS):
        sl = slice(g * B_HD, (g + 1) * B_HD)
        q_ref[:, sl] = (_head_rms(zq_ref[:, sl], qg_ref[...]) * (B_HD ** -0.5 * LOG2E)).astype(BF16)
    for g in range(B_KV_HEADS):
        sl = slice(g * B_HD, (g + 1) * B_HD)
        kn = _head_rms(zk_ref[:, sl], kg_ref[...])
        k_ref[pl.ds(g, tm, stride=B_KV_HEADS), :] = kn
        kb_ref[:, sl] = kn.astype(BF16)
        v_ref[pl.ds(g, tm, stride=B_KV_HEADS), :] = zv_ref[:, sl]
    vb_ref[...] = zv_ref[...].astype(BF16)
    half = I_Q // 2
    iq_ref[:, :half] = (zi0_ref[...] * (IDX_HD ** -0.5)).astype(BF16)
    iq_ref[:, half:] = (zi1_ref[...] * (IDX_HD ** -0.5)).astype(BF16)
    ik_ref[...] = zik_ref[...]
    ikb_ref[...] = zik_ref[...].astype(BF16)
    iw_ref[...] = ziw_ref[...] * (IDX_HEADS ** -0.5)


def _b_post(z, q_gain, k_gain):
    m = z.shape[0]
    tm = 256
    blk = lambda w, c: pl.BlockSpec((tm, w), lambda i, c=c: (i, c))
    gain = pl.BlockSpec((1, B_HD), lambda i: (0, 0))
    o1 = B_Q
    o2 = o1 + B_KV
    o3 = o2 + B_KV
    o4 = o3 + I_Q
    half = I_Q // 2
    out = lambda w: pl.BlockSpec((tm, w), lambda i: (i, 0))
    sds = lambda w, dt: jax.ShapeDtypeStruct((m, w), dt)
    lin = pl.BlockSpec((tm * B_KV_HEADS, B_HD), lambda i: (i, 0))
    lin_sds = jax.ShapeDtypeStruct((m * B_KV_HEADS, B_HD), F32)
    return pl.pallas_call(
        functools.partial(_b_post_kernel, tm=tm),
        grid=(m // tm,),
        in_specs=[blk(B_Q, 0), blk(B_KV, o1 // B_KV), blk(B_KV, o2 // B_KV),
                  blk(half, o3 // half), blk(half, o3 // half + 1),
                  blk(IDX_HD, o4 // IDX_HD), blk(LANES, o4 // IDX_HD + 1), gain, gain],
        out_specs=[out(B_Q), lin, out(B_KV), out(B_KV), out(I_Q), out(IDX_HD), out(LANES), lin, out(IDX_HD)],
        out_shape=[sds(B_Q, BF16), lin_sds, sds(B_KV, BF16), sds(B_KV, BF16),
                   sds(I_Q, BF16), sds(IDX_HD, BF16), sds(LANES, F32), lin_sds, sds(IDX_HD, F32)],
        compiler_params=_cparams(("parallel",)),
        name="dsa_post",
    )(z, z, z, z, z, z, z, q_gain.reshape(1, B_HD).astype(F32), k_gain.reshape(1, B_HD).astype(F32))


def _for_each_group(n, body):
    width = CHUNKS_PER_TRIP

    def group(jj, c):
        body(width * jj, width)
        return c

    lax.fori_loop(0, n // width, group, 0)
    done = (n // width) * width
    w = width // 2
    while w >= 1:
        take = (n - done) % (2 * w) >= w

        @pl.when(take)
        def _(done=done, w=w):
            body(done, w)
        done = done + jnp.where(take, w, 0)
        w //= 2


def _lane_fold(x, op):
    r = x[:, :LANES]
    for c in range(1, x.shape[1] // LANES):
        r = op(r, x[:, c * LANES:(c + 1) * LANES])
    return r


def _attend(streams, v_of, s_ref, mx_ref, ls_ref, acc_ref, qi, tq):
    mx_ref[...] = jnp.full(mx_ref.shape, NEG, F32)

    def score(j, count, delta=None):
        off = pl.multiple_of(j * tq, tq)
        n = count * tq
        r0 = 0
        for q, k_of, near_bias, far_bias in streams:
            rows = slice(r0, r0 + q.shape[0])
            r0 += q.shape[0]
            s = lax.dot_general(q, k_of(off, n), _NT, preferred_element_type=F32)
            b = far_bias(off, n) if delta is None else near_bias(delta, off)
            if b is not None:
                s = s + b
            s_ref[rows, pl.ds(off, n)] = s
            mx_ref[rows, :] = jnp.maximum(mx_ref[rows, :], _lane_fold(s, jnp.maximum))

    _for_each_group(jnp.maximum(qi - 1, 0), score)

    @pl.when(qi >= 1)
    def _():
        score(qi - 1, 1, 1)

    score(qi, 1, 0)

    m = jnp.max(mx_ref[...], axis=1, keepdims=True)
    mx_ref[...] = jnp.broadcast_to(m, mx_ref.shape)
    ls_ref[...] = jnp.zeros_like(ls_ref)
    acc_ref[...] = jnp.zeros_like(acc_ref)

    def pv(j, count):
        off = pl.multiple_of(j * tq, tq)
        n = count * tq
        s = s_ref[:, pl.ds(off, n)]
        m_rep = mx_ref[...]
        parts = [jnp.exp2(s[:, cc * LANES:(cc + 1) * LANES] - m_rep) for cc in range(n // LANES)]
        lsum = parts[0]
        for part in parts[1:]:
            lsum = lsum + part
        ls_ref[...] += lsum
        p = jnp.concatenate(parts, axis=1).astype(BF16)
        acc_ref[...] += jnp.dot(p, v_of(off, n), preferred_element_type=F32)

    _for_each_group(qi + 1, pv)
    return acc_ref[...] / jnp.sum(ls_ref[...], axis=1, keepdims=True)


def _diff_lambda(lp, lam_init):
    s1 = jnp.sum(lp[0:1] * lp[1:2], axis=1, keepdims=True)
    s2 = jnp.sum(lp[2:3] * lp[3:4], axis=1, keepdims=True)
    return jnp.exp(s1) - jnp.exp(s2) + lam_init


def _diff_attn_kernel(q_ref, k_ref, v_ref, bias_ref, lamp_ref, sub_ref, o_ref,
                      s_ref, mx_ref, ls_ref, acc_ref, *, lam_init, tq):
    qi = pl.program_id(2)
    streams = []
    for mp in range(2):
        sl = slice(mp * A_HD, (mp + 1) * A_HD)
        streams.append((q_ref[:, sl],
                        lambda off, n, sl=sl: k_ref[pl.ds(off, n), sl],
                        lambda delta, off, mp=mp: bias_ref[mp, delta * tq:(delta + 1) * tq, :],
                        lambda off, n: None))
    o2 = _attend(streams, lambda off, n: v_ref[pl.ds(off, n), :], s_ref, mx_ref, ls_ref, acc_ref, qi, tq)
    lam = _diff_lambda(lamp_ref[...], lam_init)
    o = o2[:tq] - lam * o2[tq:]
    o_ref[...] = (_head_rms(o, sub_ref[...]) * (1.0 - lam_init)).astype(o_ref.dtype)


def _diff_attn_prompt(q, kb, vb, bias, lam_p, subln, *, batch, t_pad, lam_init):
    tq = ATT_TILE
    nq = t_pad // tq
    hw = 2 * A_HD
    return pl.pallas_call(
        functools.partial(_diff_attn_kernel, lam_init=lam_init, tq=tq),
        grid=(batch, A_HEADS, nq),
        in_specs=[pl.BlockSpec((tq, hw), lambda b, h, i: (b * nq + i, h)),
                  pl.BlockSpec((t_pad, hw), lambda b, h, i: (b, h)),
                  pl.BlockSpec((t_pad, hw), lambda b, h, i: (b, h)),
                  pl.BlockSpec((2, 2 * tq, tq), lambda b, h, i: (h, 0, 0)),
                  pl.BlockSpec((4, A_HD), lambda b, h, i: (0, 0)),
                  pl.BlockSpec((1, hw), lambda b, h, i: (0, 0))],
        out_specs=pl.BlockSpec((tq, hw), lambda b, h, i: (b * nq + i, h)),
        out_shape=jax.ShapeDtypeStruct((batch * t_pad, A_QK), BF16),
        scratch_shapes=[pltpu.VMEM((2 * tq, t_pad), F32), pltpu.VMEM((2 * tq, LANES), F32),
                        pltpu.VMEM((2 * tq, LANES), F32), pltpu.VMEM((2 * tq, hw), F32)],
        compiler_params=_cparams(("parallel", "parallel", "arbitrary")),
        name="diff_attn_prompt",
    )(q, kb, vb, bias, lam_p.astype(F32), subln.reshape(1, hw).astype(F32))


def _sortable_key(x):
    bits = lax.bitcast_convert_type(x, I32)
    return bits ^ ((bits >> 31) & 0x7FFFFFFF)


def _kth_largest(count_ge, n_sel, shape, bits):
    zero = jnp.zeros(shape, I32)
    t = jnp.where(count_ge(zero) >= n_sel, zero, jnp.full(shape, -(1 << (bits - 1)), I32))

    def bit_body(i, t):
        cand = t + jnp.left_shift(jnp.int32(1), bits - 2 - i)
        return jnp.where(count_ge(cand) >= n_sel, cand, t)

    return lax.fori_loop(0, bits - 1, bit_body, t)


def _kth_largest_key(count_ge, n_sel, shape):
    return _kth_largest(count_ge, n_sel, shape, 32)


def _dsa_kernel(q_ref, iq_ref, iw_ref, k_ref, v_ref, ik_ref, bias_ref, o_ref,
                s_ref, mb_ref, hi_ref, lo_ref, mx_ref, ls_ref, acc_ref, *, tq, n_sel):
    qi = pl.program_id(1)
    n = pl.program_id(2)

    def keys_at(off):
        return lax.bitcast_convert_type(s_ref[0:tq, pl.ds(off, tq)], I32)

    @pl.when(n == 0)
    def _():
        row = lax.broadcasted_iota(I32, (tq, tq), 0)
        colm = lax.broadcasted_iota(I32, (tq, tq), 1)

        def score_body(j, c):
            off = pl.multiple_of(j * tq, tq)
            ikb = ik_ref[pl.ds(off, tq), :]
            acc = jnp.zeros((tq, tq), F32)
            for h in range(IDX_HEADS):
                d = lax.dot_general(iq_ref[:, h * IDX_HD:(h + 1) * IDX_HD], ikb, _NT,
                                    preferred_element_type=F32)
                acc = acc + jnp.maximum(d, 0.0) * iw_ref[:, h:h + 1]
            lim = jnp.where(j < qi, tq, 0)
            acc = jnp.where(colm <= row + lim, acc, -jnp.inf)
            key = _sortable_key(acc)
            s_ref[0:tq, pl.ds(off, tq)] = lax.bitcast_convert_type(key, F32)
            hi_ref[:, pl.ds(off, tq)] = (key >> 16).astype(I16)
            lo_ref[:, pl.ds(off, tq)] = ((key & 0xFFFF) - (1 << 15)).astype(I16)
            return c

        lax.fori_loop(0, qi + 1, score_body, 0)

        def count16(plane_ref, pred):
            def col_body(j, pc):
                blk = plane_ref[:, pl.ds(pl.multiple_of(j * tq, tq), tq)]
                ind = jnp.where(pred(blk), jnp.int16(1), jnp.int16(0))
                return pc + _lane_fold(ind, jnp.add)
            pc = lax.fori_loop(0, qi + 1, col_body, jnp.zeros((tq, LANES), I16))
            return jnp.sum(pc.astype(I32), axis=1, keepdims=True)

        t_hi = _kth_largest(lambda c: count16(hi_ref, lambda b: b >= c.astype(I16)), n_sel, (tq, 1), 16)
        t_hi16 = t_hi.astype(I16)
        need = n_sel - count16(hi_ref, lambda b: b > t_hi16)

        def tie_body(j, c):
            sl = pl.ds(pl.multiple_of(j * tq, tq), tq)
            lo_ref[:, sl] = jnp.where(hi_ref[:, sl] == t_hi16, lo_ref[:, sl], jnp.int16(-(1 << 15)))
            return c

        lax.fori_loop(0, qi + 1, tie_body, 0)
        t_lo = _kth_largest(lambda c: count16(lo_ref, lambda b: b >= c.astype(I16)), need, (tq, 1), 16)
        thr = (t_hi << 16) | (t_lo + (1 << 15))

        def mask_body(j, c):
            off = pl.multiple_of(j * tq, tq)
            mb_ref[:, pl.ds(off, tq)] = jnp.where(keys_at(off) >= thr, 0.0, NEG)
            return c

        lax.fori_loop(0, qi + 1, mask_body, 0)

    hp = DSA_HEADS_PER_PASS
    for g0 in range(0, B_GROUP, hp):
        gs = range(g0, g0 + hp)
        q = jnp.concatenate([q_ref[:, g * B_HD:(g + 1) * B_HD] for g in gs], axis=0)

        def far_bias(off, n):
            mb = mb_ref[:, pl.ds(off, n)]
            return jnp.concatenate([mb] * hp, axis=0)

        def near_bias(delta, off, gs=gs):
            mb = mb_ref[:, pl.ds(off, tq)]
            return jnp.concatenate([mb + bias_ref[g, delta * tq:(delta + 1) * tq, :] for g in gs], axis=0)

        o = _attend([(q, lambda off, n: k_ref[pl.ds(off, n), :], near_bias, far_bias)],
                    lambda off, n: v_ref[pl.ds(off, n), :], s_ref, mx_ref, ls_ref, acc_ref, qi, tq)
        for i, g in enumerate(gs):
            o_ref[:, g * B_HD:(g + 1) * B_HD] = o[i * tq:(i + 1) * tq].astype(o_ref.dtype)


def _dsa_prompt(q, iq, iw, kb, vb, ikb, bias, *, batch, t_pad, n_sel):
    tq = ATT_TILE
    hp = DSA_HEADS_PER_PASS
    nq = t_pad // tq
    gw = B_GROUP * B_HD
    return pl.pallas_call(
        functools.partial(_dsa_kernel, tq=tq, n_sel=n_sel),
        grid=(batch, nq, B_KV_HEADS),
        in_specs=[pl.BlockSpec((tq, gw), lambda b, i, n: (b * nq + i, n)),
                  pl.BlockSpec((tq, I_Q), lambda b, i, n: (b * nq + i, 0)),
                  pl.BlockSpec((tq, LANES), lambda b, i, n: (b * nq + i, 0)),
                  pl.BlockSpec((t_pad, B_HD), lambda b, i, n: (b, n)),
                  pl.BlockSpec((t_pad, B_HD), lambda b, i, n: (b, n)),
                  pl.BlockSpec((t_pad, IDX_HD), lambda b, i, n: (b, 0)),
                  pl.BlockSpec((B_GROUP, 2 * tq, tq), lambda b, i, n: (n, 0, 0))],
        out_specs=pl.BlockSpec((tq, gw), lambda b, i, n: (b * nq + i, n)),
        out_shape=jax.ShapeDtypeStruct((batch * t_pad, B_Q), BF16),
        scratch_shapes=[pltpu.VMEM((hp * tq, t_pad), F32), pltpu.VMEM((tq, t_pad), F32),
                        pltpu.VMEM((tq, t_pad), I16), pltpu.VMEM((tq, t_pad), I16),
                        pltpu.VMEM((hp * tq, LANES), F32), pltpu.VMEM((hp * tq, LANES), F32),
                        pltpu.VMEM((hp * tq, B_HD), F32)],
        compiler_params=_cparams(("parallel", "arbitrary", "arbitrary"), DSA_VMEM_LIMIT_BYTES),
        name="dsa_prompt",
    )(q, iq, iw, kb, vb, ikb, bias)


def _page_specs(comps, layer, n_pages, npp):
    def spec(i):
        return pl.BlockSpec((1, 1, PAGE_SIZE * comps, LANES),
                            lambda b, p, pt, i=i: (layer, pt[b * n_pages + p * npp + i], 0, 0))
    return [spec(i) for i in range(npp)]


def _component(ref, c, comps):
    return ref[0, 0, pl.ds(c, PAGE_SIZE, stride=comps), :].astype(BF16)


def _sample_attn_body(p, npg, npp, qbd, k_refs, v_refs, kcomps, vcomps, v_ids, knew_ref, vnew_ref,
                      blast_ref, bnew_ref, m_ref, l_ref, acc_ref, mask_of):
    groups = len(v_ids)
    rg = LANES // groups
    vw = acc_ref.shape[1]

    @pl.when(p == 0)
    def _():
        m_ref[...] = jnp.full(m_ref.shape, NEG, F32)
        l_ref[...] = jnp.zeros_like(l_ref)
        acc_ref[...] = jnp.zeros_like(acc_ref)

    def scores(xcat, bias, mask_idx):
        s = lax.dot_general(qbd, xcat, _NT, preferred_element_type=F32)
        if bias is not None:
            s = s + bias
        return mask_of(mask_idx, s)

    def update(s_blocks, v_groups):
        s = jnp.concatenate(s_blocks, axis=1) if len(s_blocks) > 1 else s_blocks[0]
        m_old = m_ref[...]
        m_new = jnp.maximum(m_old, jnp.max(s, axis=1, keepdims=True))
        alpha = jnp.exp2(m_old - m_new)
        pr = jnp.exp2(s - m_new)
        l_ref[...] = alpha * l_ref[...] + jnp.sum(pr, axis=1, keepdims=True)
        m_ref[...] = m_new
        pb = pr.astype(BF16)
        for g in range(groups):
            rows = slice(g * rg, (g + 1) * rg)
            pv = None
            for i, v_group in enumerate(v_groups):
                d = jnp.dot(pb[rows, i * PAGE_SIZE:(i + 1) * PAGE_SIZE], v_group(g),
                            preferred_element_type=F32)
                pv = d if pv is None else pv + d
            acc_ref[rows, :] = alpha[rows] * acc_ref[rows, :] + pv

    is_last = p == npg - 1
    s_blocks, v_groups = [], []
    for i in range(npp):
        xcat = jnp.concatenate([_component(k_refs[i], c, kcomps) for c in range(kcomps)], axis=1)

        def v_group(g, i=i):
            parts = [_component(v_refs[i], c, vcomps) for c in v_ids[g]]
            return parts[0] if len(parts) == 1 else jnp.concatenate(parts, axis=1)

        bias = jnp.where(is_last, 1.0, 0.0) * blast_ref[...] if i == npp - 1 else None
        s_blocks.append(scores(xcat, bias, i))
        v_groups.append(v_group)
    update(s_blocks, v_groups)

    @pl.when(is_last)
    def _():
        update([scores(knew_ref[0], bnew_ref[...], npp)],
               [lambda g: vnew_ref[0, :, g * vw:(g + 1) * vw]])


def _diff_sample_kernel(pt_ref, q_ref, *refs, npp, npg, lam_init, n_new):
    k_refs = refs[:npp]
    v_refs = refs[npp:2 * npp]
    (knew_ref, vnew_ref, blast_ref, bnew_ref, lamp_ref, sub_ref,
     o_ref, m_ref, l_ref, acc_ref) = refs[2 * npp:]
    p = pl.program_id(1)
    comps = 2 * A_HEADS
    v_ids = [(h, A_HEADS + h) for h in range(A_HEADS)]
    _sample_attn_body(p, npg, npp, q_ref[0], k_refs, v_refs, comps, comps, v_ids, knew_ref, vnew_ref,
                      blast_ref, bnew_ref, m_ref, l_ref, acc_ref, lambda i, s: s)

    @pl.when(p == npg - 1)
    def _():
        lam = _diff_lambda(lamp_ref[...], lam_init)
        o = acc_ref[...] / l_ref[...]
        hw = 2 * A_HD
        for h in range(A_HEADS):
            r = 2 * h * SUBLANES
            od = o[r:r + SUBLANES] - lam * o[r + SUBLANES:r + 2 * SUBLANES]
            od = _head_rms(od, sub_ref[...]) * (1.0 - lam_init)
            o_ref[0, :, h * hw:(h + 1) * hw] = od[:n_new]


def _diff_attn_sample(qbd, cache_k, cache_v, pt_flat, knew, vnew, blast, bnew, lam_p, subln,
                      *, layer, n_pages, lam_init, n_new):
    db = qbd.shape[0]
    npp = _pages_per_step(n_pages, DIFF_SAMPLE_PAGES)
    npg = n_pages // npp
    hw = 2 * A_HD
    comps = 2 * A_HEADS
    per_b = lambda shape: pl.BlockSpec((1,) + shape, lambda b, p, pt: (b, 0, 0))
    const = lambda shape: pl.BlockSpec(shape, lambda b, p, pt: (0, 0))
    grid_spec = pltpu.PrefetchScalarGridSpec(
        num_scalar_prefetch=1,
        grid=(db, npg),
        in_specs=([per_b((LANES, A_QK))] + _page_specs(comps, layer, n_pages, npp)
                  + _page_specs(comps, layer, n_pages, npp)
                  + [per_b((PAGE_SIZE, A_QK)), per_b((PAGE_SIZE, A_QK)),
                     const((LANES, PAGE_SIZE)), const((LANES, PAGE_SIZE)),
                     const((4, A_HD)), const((1, hw))]),
        out_specs=pl.BlockSpec((1, n_new, A_QK), lambda b, p, pt: (b, 0, 0)),
        scratch_shapes=[pltpu.VMEM((LANES, 1), F32), pltpu.VMEM((LANES, 1), F32),
                        pltpu.VMEM((LANES, hw), F32)],
    )
    return pl.pallas_call(
        functools.partial(_diff_sample_kernel, npp=npp, npg=npg, lam_init=lam_init, n_new=n_new),
        grid_spec=grid_spec,
        out_shape=jax.ShapeDtypeStruct((db, n_new, A_QK), F32),
        compiler_params=_cparams(("parallel", "arbitrary")),
        name="diff_attn_sample",
    )(pt_flat, qbd, *([cache_k] * npp), *([cache_v] * npp), knew, vnew, blast, bnew,
      lam_p.astype(F32), subln.reshape(1, hw).astype(F32))


def _idx_sample_kernel(pt_ref, iq_ref, w_ref, *refs, npp, npg, n_pages, n_sel, n_new):
    ik_refs = refs[:npp]
    iknew_ref, keys_ref, thr_ref = refs[npp:]
    p = pl.program_id(1)
    iq = iq_ref[0]
    w = w_ref[0]

    def slot_scores(ik):
        d = lax.dot_general(iq, ik, _NT, preferred_element_type=F32)
        r = jnp.maximum(d, 0.0) * w
        return jnp.sum(r.reshape(SUBLANES, IDX_HEADS, PAGE_SIZE), axis=1)

    for i in range(npp):
        sc = slot_scores(ik_refs[i][0, 0].astype(BF16))
        off = pl.multiple_of((p * npp + i) * PAGE_SIZE, PAGE_SIZE)
        keys_ref[0, :, pl.ds(off, PAGE_SIZE)] = _sortable_key(sc)

    @pl.when(p == npg - 1)
    def _():
        sc = slot_scores(iknew_ref[0])
        srow = lax.broadcasted_iota(I32, sc.shape, 0)
        tcol = lax.broadcasted_iota(I32, sc.shape, 1)
        sc = jnp.where((tcol <= srow) & (tcol < n_new), sc, -jnp.inf)
        keys_ref[0, :, n_pages * PAGE_SIZE:(n_pages + 1) * PAGE_SIZE] = _sortable_key(sc)

        def count_ge(cand):
            return jnp.sum(jnp.where(keys_ref[0] >= cand, 1, 0), axis=1, keepdims=True)

        t = _kth_largest_key(count_ge, n_sel, (SUBLANES, 1))
        thr_ref[0] = jnp.broadcast_to(t, (SUBLANES, LANES))


def _idx_sample(iqs, ws, cache_ki, pt_flat, iknew, *, layer, n_pages, n_sel, n_new):
    db = iqs.shape[0]
    npp = _pages_per_step(n_pages, IDX_SAMPLE_PAGES)
    npg = n_pages // npp
    width = (n_pages + 1) * PAGE_SIZE
    per_b = lambda shape: pl.BlockSpec((1,) + shape, lambda b, p, pt: (b, 0, 0))
    grid_spec = pltpu.PrefetchScalarGridSpec(
        num_scalar_prefetch=1,
        grid=(db, npg),
        in_specs=([per_b((SUBLANES * IDX_HEADS, IDX_HD)), per_b((SUBLANES * IDX_HEADS, 1))]
                  + _page_specs(1, layer, n_pages, npp) + [per_b((PAGE_SIZE, IDX_HD))]),
        out_specs=[per_b((SUBLANES, width)), per_b((SUBLANES, LANES))],
    )
    return pl.pallas_call(
        functools.partial(_idx_sample_kernel, npp=npp, npg=npg, n_pages=n_pages, n_sel=n_sel, n_new=n_new),
        grid_spec=grid_spec,
        out_shape=[jax.ShapeDtypeStruct((db, SUBLANES, width), I32),
                   jax.ShapeDtypeStruct((db, SUBLANES, LANES), I32)],
        compiler_params=_cparams(("parallel", "arbitrary")),
        name="dsa_index_sample",
    )(pt_flat, iqs, ws, *([cache_ki] * npp), iknew)


def _dsa_sample_kernel(pt_ref, q_ref, *refs, npp, npg, n_new):
    k_refs = refs[:npp]
    v_refs = refs[npp:2 * npp]
    key_refs = refs[2 * npp:3 * npp + 1]
    (thr_ref, knew_ref, vnew_ref, blast_ref, bnew_ref,
     o_ref, m_ref, l_ref, acc_ref) = refs[3 * npp + 1:]
    p = pl.program_id(1)

    def mask_of(i, s):
        keys = jnp.concatenate([key_refs[i][0]] * B_HEADS, axis=0)
        thr = jnp.concatenate([thr_ref[0]] * B_HEADS, axis=0)
        return jnp.where(keys >= thr, s, NEG)

    v_ids = [(n,) for n in range(B_KV_HEADS)]
    _sample_attn_body(p, npg, npp, q_ref[0], k_refs, v_refs, B_KV_HEADS, B_KV_HEADS, v_ids, knew_ref,
                      vnew_ref, blast_ref, bnew_ref, m_ref, l_ref, acc_ref, mask_of)

    @pl.when(p == npg - 1)
    def _():
        o = acc_ref[...] / l_ref[...]
        for hd in range(B_HEADS):
            o_ref[0, :, hd * B_HD:(hd + 1) * B_HD] = o[hd * SUBLANES:hd * SUBLANES + n_new]


def _dsa_attn_sample(qbd, cache_k, cache_v, keys, thr, pt_flat, knew, vnew, blast, bnew,
                     *, layer, n_pages, n_new):
    db = qbd.shape[0]
    npp = _pages_per_step(n_pages, DSA_SAMPLE_PAGES)
    npg = n_pages // npp
    per_b = lambda shape: pl.BlockSpec((1,) + shape, lambda b, p, pt: (b, 0, 0))
    const = lambda shape: pl.BlockSpec(shape, lambda b, p, pt: (0, 0))
    key_specs = [pl.BlockSpec((1, SUBLANES, PAGE_SIZE), lambda b, p, pt, i=i: (b, 0, p * npp + i))
                 for i in range(npp)]
    key_specs.append(pl.BlockSpec((1, SUBLANES, PAGE_SIZE), lambda b, p, pt: (b, 0, n_pages)))
    grid_spec = pltpu.PrefetchScalarGridSpec(
        num_scalar_prefetch=1,
        grid=(db, npg),
        in_specs=([per_b((LANES, B_KV))] + _page_specs(B_KV_HEADS, layer, n_pages, npp)
                  + _page_specs(B_KV_HEADS, layer, n_pages, npp) + key_specs
                  + [per_b((SUBLANES, LANES)), per_b((PAGE_SIZE, B_KV)), per_b((PAGE_SIZE, B_KV)),
                     const((LANES, PAGE_SIZE)), const((LANES, PAGE_SIZE))]),
        out_specs=pl.BlockSpec((1, n_new, B_Q), lambda b, p, pt: (b, 0, 0)),
        scratch_shapes=[pltpu.VMEM((LANES, 1), F32), pltpu.VMEM((LANES, 1), F32),
                        pltpu.VMEM((LANES, B_HD), F32)],
    )
    return pl.pallas_call(
        functools.partial(_dsa_sample_kernel, npp=npp, npg=npg, n_new=n_new),
        grid_spec=grid_spec,
        out_shape=jax.ShapeDtypeStruct((db, n_new, B_Q), F32),
        compiler_params=_cparams(("parallel", "arbitrary")),
        name="dsa_attn_sample",
    )(pt_flat, qbd, *([cache_k] * npp), *([cache_v] * npp), *([keys] * (npp + 1)), thr, knew, vnew,
      blast, bnew)


def _pad_rows(x, rows):
    return jnp.pad(x, ((0, 0), (0, rows - x.shape[1]), (0, 0)))


def kernel(x_prompt, x_sample, cache_a_k, cache_a_v, cache_b_k, cache_b_v, cache_b_kidx, page_table, meta_tokens, rel_bias_table, norm_mix, norm_ffn, a_w_in, a_w_out, a_q_norm, a_k_norm, a_lambda, a_subln, b_w_in, b_w_out, b_q_norm, b_k_norm, ffn_w1, ffn_w2):
    batch, seq, d_model = x_prompt.shape
    db, ds, _ = x_sample.shape
    depth = norm_mix.shape[0]
    n_pages = page_table.shape[1]
    past_len = n_pages * PAGE_SIZE
    t_real = seq + N_META
    t_pad = -(-t_real // Q_BLOCK) * Q_BLOCK
    tq = ATT_TILE
    n_s = db * ds
    r0 = batch * t_pad
    m_tot = r0 + TAIL_ROWS
    assert t_pad % tq == 0 and r0 % TAIL_ROWS == 0 and n_s <= TAIL_ROWS and ds <= SUBLANES
    n_sel_prompt = min(TOPK_MAX, seq // 4)
    n_sel_sample = min(TOPK_MAX, (past_len + ds) // 4)

    meta = jnp.broadcast_to(meta_tokens.astype(F32), (batch, N_META, d_model))
    xp = jnp.concatenate([meta, x_prompt, jnp.zeros((batch, t_pad - t_real, d_model), F32)], axis=1)
    x = jnp.concatenate([xp.reshape(r0, d_model), x_sample.reshape(n_s, d_model),
                         jnp.zeros((TAIL_ROWS - n_s, d_model), F32)], axis=0)

    ri = lax.broadcasted_iota(I32, (tq, tq), 0)
    ci = lax.broadcasted_iota(I32, (tq, tq), 1)
    r8 = lax.broadcasted_iota(I32, (SUBLANES, tq), 0)
    c8 = lax.broadcasted_iota(I32, (SUBLANES, tq), 1)
    dist = jnp.concatenate([ri - ci, tq + ri - ci, PAGE_SIZE + r8 - c8, r8 - c8], axis=0)
    bias_all = _bias_tiles(rel_bias_table, dist)
    bias_prompt = bias_all[:, :2 * tq]
    b_last = bias_all[:, 2 * tq:2 * tq + SUBLANES, :PAGE_SIZE].reshape(N_BIAS_HEADS * SUBLANES, PAGE_SIZE)
    b_new = bias_all[:, 2 * tq + SUBLANES:2 * tq + 2 * SUBLANES, :PAGE_SIZE].reshape(
        N_BIAS_HEADS * SUBLANES, PAGE_SIZE)

    pt_flat = page_table.reshape(-1).astype(I32)
    n_a, n_pool = cache_a_k.shape[:2]
    n_b = cache_b_k.shape[0]
    cache_a_k2 = cache_a_k.reshape(n_a, n_pool, PAGE_SIZE * 2 * A_HEADS, A_HD)
    cache_a_v2 = cache_a_v.reshape(n_a, n_pool, PAGE_SIZE, A_HEADS, 2, A_HD).transpose(
        0, 1, 2, 4, 3, 5).reshape(n_a, n_pool, PAGE_SIZE * 2 * A_HEADS, A_HD)
    cache_b_k2 = cache_b_k.reshape(n_b, n_pool, PAGE_SIZE * B_KV_HEADS, B_HD)
    cache_b_v2 = cache_b_v.reshape(n_b, n_pool, PAGE_SIZE * B_KV_HEADS, B_HD)

    b_in_pad = -(-B_IN // 256) * 256
    eye_m = jnp.eye(2 * A_HEADS, dtype=BF16)
    eye_n = jnp.eye(B_KV_HEADS, dtype=BF16)

    def rows_of(a):
        return a[r0:r0 + n_s].reshape(db, ds, a.shape[1])

    def prompt_rows(a):
        return a[:r0].reshape((batch, t_pad) + a.shape[1:])[:, :t_real]

    def sample_rows(a):
        return a[r0:r0 + n_s].reshape((db, ds) + a.shape[1:])

    outs = {k: [] for k in ("akp", "avp", "bkp", "bvp", "bip", "aks", "avs", "bks", "bvs", "bis")}
    for i in range(depth):
        j = i // N_MIXERS
        h = _rmsnorm(x, norm_mix[i])
        if i % N_MIXERS == 0:
            lam_init = 0.8 - 0.6 * math.exp(-0.3 * i)
            z = _matmul(h, a_w_in[j].astype(BF16))
            q, k_lin, kb, vb, v_lin = _a_post(z, a_q_norm[j], a_k_norm[j])
            k = k_lin.reshape(m_tot, A_HEADS, 2, A_HD)
            v = v_lin.reshape(m_tot, 2, A_HEADS, A_HD).transpose(0, 2, 1, 3).reshape(m_tot, A_HEADS, 2 * A_HD)
            o_p = _diff_attn_prompt(q, kb, vb, bias_prompt, a_lambda[j], a_subln[j],
                                    batch=batch, t_pad=t_pad, lam_init=lam_init)
            qs = _pad_rows(rows_of(q), SUBLANES).reshape(db, SUBLANES, 2 * A_HEADS, A_HD)
            qbd = (qs[:, None] * eye_m[None, :, None, :, None]).reshape(db, LANES, A_QK)
            o_s = _diff_attn_sample(qbd, cache_a_k2, cache_a_v2, pt_flat,
                                    _pad_rows(rows_of(kb), PAGE_SIZE), _pad_rows(rows_of(vb), PAGE_SIZE),
                                    b_last, b_new, a_lambda[j], a_subln[j],
                                    layer=j, n_pages=n_pages, lam_init=lam_init, n_new=ds)
            w_out = a_w_out[j]
            outs["akp"].append(prompt_rows(k))
            outs["avp"].append(prompt_rows(v))
            outs["aks"].append(sample_rows(k))
            outs["avs"].append(sample_rows(v))
        else:
            w_in = jnp.pad(b_w_in[j], ((0, 0), (0, b_in_pad - B_IN))).astype(BF16)
            z = _matmul(h, w_in)
            q, k_lin, kb, vb, iq, ikb, iw, v_lin, ik = _b_post(z, b_q_norm[j], b_k_norm[j])
            k = k_lin.reshape(m_tot, B_KV_HEADS, B_HD)
            v = v_lin.reshape(m_tot, B_KV_HEADS, B_HD)
            o_p = _dsa_prompt(q, iq, iw, kb, vb, ikb, bias_prompt,
                              batch=batch, t_pad=t_pad, n_sel=n_sel_prompt)
            iqs = _pad_rows(rows_of(iq), SUBLANES).reshape(db, SUBLANES * IDX_HEADS, IDX_HD)
            ws = _pad_rows(rows_of(iw)[:, :, :IDX_HEADS], SUBLANES).reshape(db, SUBLANES * IDX_HEADS, 1)
            keys, thr = _idx_sample(iqs, ws, cache_b_kidx, pt_flat, _pad_rows(rows_of(ikb), PAGE_SIZE),
                                    layer=j, n_pages=n_pages, n_sel=n_sel_sample, n_new=ds)
            qs = _pad_rows(rows_of(q), SUBLANES).reshape(db, SUBLANES, B_KV_HEADS, B_GROUP, B_HD)
            qbd = (qs.transpose(0, 3, 1, 2, 4)[:, None]
                   * eye_n[None, :, None, None, :, None]).reshape(db, LANES, B_KV)
            o_s = _dsa_attn_sample(qbd, cache_b_k2, cache_b_v2, keys, thr, pt_flat,
                                   _pad_rows(rows_of(kb), PAGE_SIZE), _pad_rows(rows_of(vb), PAGE_SIZE),
                                   b_last, b_new, layer=j, n_pages=n_pages, n_new=ds)
            w_out = b_w_out[j]
            outs["bkp"].append(prompt_rows(k))
            outs["bvp"].append(prompt_rows(v))
            outs["bip"].append(prompt_rows(ik))
            outs["bks"].append(sample_rows(k))
            outs["bvs"].append(sample_rows(v))
            outs["bis"].append(sample_rows(ik))
        width = o_p.shape[1]
        o = jnp.concatenate([o_p, o_s.reshape(n_s, width).astype(BF16),
                             jnp.zeros((TAIL_ROWS - n_s, width), BF16)], axis=0)
        x = _matmul(o, w_out.astype(BF16), mode="res", res=x)
        hf = _rmsnorm(x, norm_ffn[i])
        u = _matmul(hf, ffn_w1[i].astype(BF16), mode="relu2", out_dtype=BF16)
        x = _matmul(u, ffn_w2[i].astype(BF16), mode="res", res=x)

    y_prompt = x[:r0].reshape(batch, t_pad, d_model)[:, N_META:t_real]
    y_sample = x[r0:r0 + n_s].reshape(db, ds, d_model)
    st = lambda key: jnp.stack(outs[key])
    return (y_prompt, y_sample, st("akp"), st("avp"), st("bkp"), st("bvp"), st("bip"),
            st("aks"), st("avs"), st("bks"), st("bvs"), st("bis"))
```

```python
import functools
import math

import jax
import jax.numpy as jnp
from jax import lax
from jax.experimental import pallas as pl
from jax.experimental.pallas import tpu as pltpu

F32 = jnp.float32
BF16 = jnp.bfloat16
I32 = jnp.int32

N_META = 16
N_MIXERS = 2
A_HEADS = 8
A_HD = 128
A_QK = A_HEADS * 2 * A_HD
B_HEADS = 16
B_KV_HEADS = 4
B_GROUP = B_HEADS // B_KV_HEADS
B_HD = 128
B_Q = B_HEADS * B_HD
B_KV = B_KV_HEADS * B_HD
IDX_HEADS = 16
IDX_HD = 128
I_Q = IDX_HEADS * IDX_HD
B_IN = B_Q + 2 * B_KV + I_Q + IDX_HD + IDX_HEADS
TOPK_MAX = 256
N_BUCKETS = 32
MAX_DISTANCE = 128
N_BIAS_HEADS = 16
PAGE_SIZE = 128
Q_BLOCK = 128
EPS = 1e-6

LANES = 128
SUBLANES = 8
ATT_TILE = 384
POST_TILE = 128
TAIL_ROWS = 256
NEG = -1e30
INT_MIN = -2 ** 31
LOG2E = math.log2(math.e)
DIFF_SAMPLE_PAGES = 4
DSA_SAMPLE_PAGES = 8
IDX_SAMPLE_PAGES = 16
CHUNKS_PER_TRIP = 4
DSA_HEADS_PER_PASS = 2
VMEM_LIMIT_BYTES = 48 * 1024 * 1024
DSA_VMEM_LIMIT_BYTES = 58 * 1024 * 1024
I16 = jnp.int16

_NT = (((1,), (1,)), ((), ()))


def _cparams(sem, vmem_limit=VMEM_LIMIT_BYTES):
    return pltpu.CompilerParams(dimension_semantics=sem, vmem_limit_bytes=vmem_limit)


def _pages_per_step(n_pages, want):
    return max(c for c in range(1, want + 1) if n_pages % c == 0)


def _pick(n, candidates):
    for c in candidates:
        if n % c == 0:
            return c
    raise ValueError(f"no tile for {n} in {candidates}")


def _rms_kernel(x_ref, g_ref, o_ref):
    x = x_ref[...]
    ms = jnp.mean(x * x, axis=-1, keepdims=True)
    o_ref[...] = (x * lax.rsqrt(ms + EPS) * g_ref[...]).astype(o_ref.dtype)


def _rmsnorm(x, g):
    m, d = x.shape
    tm = _pick(m, (512, 256))
    return pl.pallas_call(
        _rms_kernel,
        grid=(m // tm,),
        in_specs=[pl.BlockSpec((tm, d), lambda i: (i, 0)),
                  pl.BlockSpec((1, d), lambda i: (0, 0))],
        out_specs=pl.BlockSpec((tm, d), lambda i: (i, 0)),
        out_shape=jax.ShapeDtypeStruct((m, d), BF16),
        compiler_params=_cparams(("parallel",)),
        name="rmsnorm",
    )(x, g.reshape(1, d).astype(F32))


def _mm_kernel(*refs, nk, mode):
    if mode == "res":
        a_ref, w_ref, r_ref, o_ref, acc_ref = refs
    else:
        a_ref, w_ref, o_ref, acc_ref = refs
        r_ref = None
    k = pl.program_id(2)

    @pl.when(k == 0)
    def _():
        acc_ref[...] = jnp.zeros_like(acc_ref)

    acc_ref[...] += jnp.dot(a_ref[...], w_ref[...], preferred_element_type=F32)

    @pl.when(k == nk - 1)
    def _():
        acc = acc_ref[...]
        if mode == "relu2":
            r = jnp.maximum(acc, 0.0)
            acc = r * r
        elif mode == "res":
            acc = acc + r_ref[...]
        o_ref[...] = acc.astype(o_ref.dtype)


def _matmul(a, w, layer, *, mode="plain", res=None, out_dtype=F32):
    m, kd = a.shape
    n = w.shape[2]
    tm = _pick(m, (1088, 512, 256))
    tn = _pick(n, (1024, 768, 512, 256, 128))
    tk = _pick(kd, (2048, 1024, 512, 256))
    nk = kd // tk
    in_specs = [pl.BlockSpec((tm, tk), lambda i, j, k: (i, k)),
                pl.BlockSpec((None, tk, tn), lambda i, j, k: (layer, k, j))]
    args = [a, w]
    if mode == "res":
        in_specs.append(pl.BlockSpec((tm, tn), lambda i, j, k: (i, j)))
        args.append(res)
    return pl.pallas_call(
        functools.partial(_mm_kernel, nk=nk, mode=mode),
        grid=(m // tm, n // tn, nk),
        in_specs=in_specs,
        out_specs=pl.BlockSpec((tm, tn), lambda i, j, k: (i, j)),
        out_shape=jax.ShapeDtypeStruct((m, n), out_dtype),
        scratch_shapes=[pltpu.VMEM((tm, tn), F32)],
        compiler_params=_cparams(("parallel", "parallel", "arbitrary")),
        name="matmul_" + mode,
    )(*args)


def _bias_kernel(tab_ref, d_ref, o_ref):
    h = pl.program_id(0)
    d = d_ref[...]
    max_exact = N_BUCKETS // 2
    n = jnp.maximum(d, 0)
    nf = jnp.maximum(n, 1).astype(F32)
    large = max_exact + (jnp.log(nf / max_exact) / math.log(MAX_DISTANCE / max_exact)
                         * (N_BUCKETS - max_exact)).astype(I32)
    bucket = jnp.where(n < max_exact, n, jnp.minimum(large, N_BUCKETS - 1))
    acc = jnp.zeros(d.shape, F32)
    for b in range(N_BUCKETS):
        acc = jnp.where(bucket == b, tab_ref[b * N_BIAS_HEADS + h], acc)
    far = tab_ref[(N_BUCKETS - 1) * N_BIAS_HEADS + h]
    o_ref[0] = jnp.where(d < 0, NEG, (acc - far) * LOG2E)


def _bias_tiles(table, dist):
    r, c = dist.shape
    return pl.pallas_call(
        _bias_kernel,
        grid=(N_BIAS_HEADS,),
        in_specs=[pl.BlockSpec(memory_space=pltpu.SMEM),
                  pl.BlockSpec((r, c), lambda h: (0, 0))],
        out_specs=pl.BlockSpec((1, r, c), lambda h: (h, 0, 0)),
        out_shape=jax.ShapeDtypeStruct((N_BIAS_HEADS, r, c), F32),
        compiler_params=_cparams(("arbitrary",)),
        name="bias_tiles",
    )(table.astype(F32).reshape(-1), dist)


def _head_rms(x, g):
    ms = jnp.mean(x * x, axis=-1, keepdims=True)
    return x * lax.rsqrt(ms + EPS) * g


def _post_specs(kind, batch, t_pad, t_real, r0, layers, layer):
    tm = POST_TILE
    tiles = t_pad // tm
    whole = t_real // tm
    rem = t_real - whole * tm
    state_rows, state_tiles = tm, None
    state_sds = lambda comps: jax.ShapeDtypeStruct((layers, batch, t_real * comps, LANES), F32)
    if kind == "prompt":
        grid = (batch, tiles)
        rows = lambda b, t: b * tiles + t
        n_rows = batch * t_pad
        state_tiles = whole
        state_spec = lambda comps: pl.BlockSpec((None, None, tm * comps, LANES),
                                                lambda b, t: (layer, b, jnp.minimum(t, whole - 1), 0))
    elif kind == "edge":
        assert rem % 16 == 0 and (whole * tm) % rem == 0
        grid = (batch, 1)
        rows = lambda b, t: b * tiles + whole
        n_rows = batch * tm
        state_rows = rem
        state_spec = lambda comps: pl.BlockSpec((None, None, rem * comps, LANES),
                                                lambda b, t: (layer, b, whole * tm // rem, 0))
    else:
        grid = (1, TAIL_ROWS // tm)
        rows = lambda b, t: r0 // tm + t
        n_rows = TAIL_ROWS
        state_spec = lambda comps: pl.BlockSpec((tm * comps, LANES), lambda b, t: (t, 0))
        state_sds = lambda comps: jax.ShapeDtypeStruct((TAIL_ROWS * comps, LANES), F32)
    in_blk = lambda w, c: pl.BlockSpec((tm, w), lambda b, t, c=c: (rows(b, t), c))
    out_blk = lambda w: pl.BlockSpec((tm, w), lambda b, t: (b * grid[1] + t, 0))
    out_sds = lambda w, dt: jax.ShapeDtypeStruct((n_rows, w), dt)
    return grid, in_blk, out_blk, out_sds, state_spec, state_sds, state_rows, state_tiles


def _when_state_tile(state_tiles, fn):
    if state_tiles is None:
        fn()
    else:
        pl.when(pl.program_id(1) < state_tiles)(fn)


def _a_post_kernel(zq_ref, zk_ref, zv_ref, qg_ref, kg_ref, *rest, state_rows, state_tiles):
    q_ref, k_ref, kb_ref, vb_ref, v_ref = rest[-5:]
    scale = A_HD ** -0.5 * LOG2E
    comps = 2 * A_HEADS
    for g in range(comps):
        sl = slice(g * A_HD, (g + 1) * A_HD)
        q_ref[:, sl] = (_head_rms(zq_ref[:, sl], qg_ref[...]) * scale).astype(BF16)
        kn = _head_rms(zk_ref[:, sl], kg_ref[...])
        kb_ref[:, sl] = kn.astype(BF16)
        head, half = g // 2, g % 2

        def store_state(g=g, sl=sl, kn=kn, c_v=half * A_HEADS + head):
            k_ref[pl.ds(g, state_rows, stride=comps), :] = kn[:state_rows]
            v_ref[pl.ds(c_v, state_rows, stride=comps), :] = zv_ref[:state_rows, sl]

        _when_state_tile(state_tiles, store_state)
    vb_ref[...] = zv_ref[...].astype(BF16)


def _a_post(z, q_gain, k_gain, kind, dims, states=()):
    comps = 2 * A_HEADS
    grid, in_blk, out_blk, out_sds, state_spec, state_sds, state_rows, state_tiles = _post_specs(kind, *dims)
    gain = pl.BlockSpec((1, A_HD), lambda b, t: (0, 0))
    return pl.pallas_call(
        functools.partial(_a_post_kernel, state_rows=state_rows, state_tiles=state_tiles),
        grid=grid,
        in_specs=([in_blk(A_QK, 0), in_blk(A_QK, 1), in_blk(A_QK, 2), gain, gain]
                  + [pl.BlockSpec(memory_space=pl.ANY)] * len(states)),
        out_specs=[out_blk(A_QK), state_spec(comps), out_blk(A_QK), out_blk(A_QK), state_spec(comps)],
        out_shape=[out_sds(A_QK, BF16), state_sds(comps), out_sds(A_QK, BF16), out_sds(A_QK, BF16),
                   state_sds(comps)],
        input_output_aliases=({5: 1, 6: 4} if states else {}),
        compiler_params=_cparams(("parallel", "parallel")),
        name="diff_post_" + kind,
    )(z, z, z, q_gain.reshape(1, A_HD).astype(F32), k_gain.reshape(1, A_HD).astype(F32), *states)


def _b_post_kernel(zq_ref, zk_ref, zv_ref, zi0_ref, zi1_ref, zik_ref, ziw_ref, qg_ref, kg_ref, *rest,
                   state_rows, state_tiles):
    q_ref, k_ref, kb_ref, vb_ref, iq_ref, ikb_ref, iw_ref, v_ref, ik_ref = rest[-9:]
    for g in range(B_HEADS):
        sl = slice(g * B_HD, (g + 1) * B_HD)
        q_ref[:, sl] = (_head_rms(zq_ref[:, sl], qg_ref[...]) * (B_HD ** -0.5 * LOG2E)).astype(BF16)
    for g in range(B_KV_HEADS):
        sl = slice(g * B_HD, (g + 1) * B_HD)
        kn = _head_rms(zk_ref[:, sl], kg_ref[...])
        kb_ref[:, sl] = kn.astype(BF16)

        def store_state(g=g, sl=sl, kn=kn):
            k_ref[pl.ds(g, state_rows, stride=B_KV_HEADS), :] = kn[:state_rows]
            v_ref[pl.ds(g, state_rows, stride=B_KV_HEADS), :] = zv_ref[:state_rows, sl]

        _when_state_tile(state_tiles, store_state)
    vb_ref[...] = zv_ref[...].astype(BF16)
    half = I_Q // 2
    iq_ref[:, :half] = (zi0_ref[...] * (IDX_HD ** -0.5)).astype(BF16)
    iq_ref[:, half:] = (zi1_ref[...] * (IDX_HD ** -0.5)).astype(BF16)

    def store_index_key():
        ik_ref[...] = zik_ref[:state_rows, :]

    _when_state_tile(state_tiles, store_index_key)
    ikb_ref[...] = zik_ref[...].astype(BF16)
    iw_ref[...] = ziw_ref[...] * (IDX_HEADS ** -0.5)


def _b_post(z, q_gain, k_gain, kind, dims, states=()):
    grid, blk, out, sds, state_spec, state_sds, state_rows, state_tiles = _post_specs(kind, *dims)
    gain = pl.BlockSpec((1, B_HD), lambda b, t: (0, 0))
    o1 = B_Q
    o2 = o1 + B_KV
    o3 = o2 + B_KV
    o4 = o3 + I_Q
    half = I_Q // 2
    kv = B_KV_HEADS
    return pl.pallas_call(
        functools.partial(_b_post_kernel, state_rows=state_rows, state_tiles=state_tiles),
        grid=grid,
        in_specs=([blk(B_Q, 0), blk(B_KV, o1 // B_KV), blk(B_KV, o2 // B_KV),
                   blk(half, o3 // half), blk(half, o3 // half + 1),
                   blk(IDX_HD, o4 // IDX_HD), blk(LANES, o4 // IDX_HD + 1), gain, gain]
                  + [pl.BlockSpec(memory_space=pl.ANY)] * len(states)),
        out_specs=[out(B_Q), state_spec(kv), out(B_KV), out(B_KV), out(I_Q), out(IDX_HD), out(LANES),
                   state_spec(kv), state_spec(1)],
        out_shape=[sds(B_Q, BF16), state_sds(kv), sds(B_KV, BF16), sds(B_KV, BF16),
                   sds(I_Q, BF16), sds(IDX_HD, BF16), sds(LANES, F32), state_sds(kv), state_sds(1)],
        input_output_aliases=({9: 1, 10: 7, 11: 8} if states else {}),
        compiler_params=_cparams(("parallel", "parallel")),
        name="dsa_post_" + kind,
    )(z, z, z, z, z, z, z, q_gain.reshape(1, B_HD).astype(F32), k_gain.reshape(1, B_HD).astype(F32), *states)


def _for_each_group(n, body):
    width = CHUNKS_PER_TRIP

    def group(jj, c):
        body(width * jj, width)
        return c

    lax.fori_loop(0, n // width, group, 0)
    done = (n // width) * width
    w = width // 2
    while w >= 1:
        take = (n - done) % (2 * w) >= w

        @pl.when(take)
        def _(done=done, w=w):
            body(done, w)
        done = done + jnp.where(take, w, 0)
        w //= 2


def _lane_fold(x, op):
    r = x[:, :LANES]
    for c in range(1, x.shape[1] // LANES):
        r = op(r, x[:, c * LANES:(c + 1) * LANES])
    return r


def _attend(streams, v_of, s_ref, mx_ref, ls_ref, acc_ref, qi, tq):
    mx_ref[...] = jnp.full(mx_ref.shape, NEG, F32)

    def score(j, count, delta=None):
        off = pl.multiple_of(j * tq, tq)
        n = count * tq
        r0 = 0
        for q, k_of, near_bias, far_bias in streams:
            rows = slice(r0, r0 + q.shape[0])
            r0 += q.shape[0]
            s = lax.dot_general(q, k_of(off, n), _NT, preferred_element_type=F32)
            b = far_bias(off, n) if delta is None else near_bias(delta, off)
            if b is not None:
                s = s + b
            s_ref[rows, pl.ds(off, n)] = s
            mx_ref[rows, :] = jnp.maximum(mx_ref[rows, :], _lane_fold(s, jnp.maximum))

    _for_each_group(jnp.maximum(qi - 1, 0), score)

    @pl.when(qi >= 1)
    def _():
        score(qi - 1, 1, 1)

    score(qi, 1, 0)

    m = jnp.max(mx_ref[...], axis=1, keepdims=True)
    mx_ref[...] = jnp.broadcast_to(m, mx_ref.shape)
    ls_ref[...] = jnp.zeros_like(ls_ref)
    acc_ref[...] = jnp.zeros_like(acc_ref)

    def pv(j, count):
        off = pl.multiple_of(j * tq, tq)
        n = count * tq
        s = s_ref[:, pl.ds(off, n)]
        m_rep = mx_ref[...]
        parts = [jnp.exp2(s[:, cc * LANES:(cc + 1) * LANES] - m_rep) for cc in range(n // LANES)]
        lsum = parts[0]
        for part in parts[1:]:
            lsum = lsum + part
        ls_ref[...] += lsum
        p = jnp.concatenate(parts, axis=1).astype(BF16)
        acc_ref[...] += jnp.dot(p, v_of(off, n), preferred_element_type=F32)

    _for_each_group(qi + 1, pv)
    return acc_ref[...] / jnp.sum(ls_ref[...], axis=1, keepdims=True)


def _diff_lambda(lp, lam_init):
    s1 = jnp.sum(lp[0:1] * lp[1:2], axis=1, keepdims=True)
    s2 = jnp.sum(lp[2:3] * lp[3:4], axis=1, keepdims=True)
    return jnp.exp(s1) - jnp.exp(s2) + lam_init


def _diff_attn_kernel(q_ref, k_ref, v_ref, bias_ref, lamp_ref, sub_ref, o_ref,
                      s_ref, mx_ref, ls_ref, acc_ref, *, lam_init, tq):
    qi = pl.program_id(2)
    streams = []
    for mp in range(2):
        sl = slice(mp * A_HD, (mp + 1) * A_HD)
        streams.append((q_ref[:, sl],
                        lambda off, n, sl=sl: k_ref[pl.ds(off, n), sl],
                        lambda delta, off, mp=mp: bias_ref[mp, delta * tq:(delta + 1) * tq, :],
                        lambda off, n: None))
    o2 = _attend(streams, lambda off, n: v_ref[pl.ds(off, n), :], s_ref, mx_ref, ls_ref, acc_ref, qi, tq)
    lam = _diff_lambda(lamp_ref[...], lam_init)
    o = o2[:tq] - lam * o2[tq:]
    o_ref[...] = (_head_rms(o, sub_ref[...]) * (1.0 - lam_init)).astype(o_ref.dtype)


def _diff_attn_prompt(q, kb, vb, bias, lam_p, subln, *, batch, t_pad, lam_init):
    tq = ATT_TILE
    nq = t_pad // tq
    hw = 2 * A_HD
    return pl.pallas_call(
        functools.partial(_diff_attn_kernel, lam_init=lam_init, tq=tq),
        grid=(batch, A_HEADS, nq),
        in_specs=[pl.BlockSpec((tq, hw), lambda b, h, i: (b * nq + i, h)),
                  pl.BlockSpec((t_pad, hw), lambda b, h, i: (b, h)),
                  pl.BlockSpec((t_pad, hw), lambda b, h, i: (b, h)),
                  pl.BlockSpec((2, 2 * tq, tq), lambda b, h, i: (h, 0, 0)),
                  pl.BlockSpec((4, A_HD), lambda b, h, i: (0, 0)),
                  pl.BlockSpec((1, hw), lambda b, h, i: (0, 0))],
        out_specs=pl.BlockSpec((tq, hw), lambda b, h, i: (b * nq + i, h)),
        out_shape=jax.ShapeDtypeStruct((batch * t_pad, A_QK), BF16),
        scratch_shapes=[pltpu.VMEM((2 * tq, t_pad), F32), pltpu.VMEM((2 * tq, LANES), F32),
                        pltpu.VMEM((2 * tq, LANES), F32), pltpu.VMEM((2 * tq, hw), F32)],
        compiler_params=_cparams(("parallel", "parallel", "arbitrary")),
        name="diff_attn_prompt",
    )(q, kb, vb, bias, lam_p.astype(F32), subln.reshape(1, hw).astype(F32))


def _sortable_key(x):
    bits = lax.bitcast_convert_type(x, I32)
    return bits ^ ((bits >> 31) & 0x7FFFFFFF)


def _kth_largest(count_ge, n_sel, shape, bits):
    zero = jnp.zeros(shape, I32)
    t = jnp.where(count_ge(zero) >= n_sel, zero, jnp.full(shape, -(1 << (bits - 1)), I32))

    def bit_body(i, t):
        cand = t + jnp.left_shift(jnp.int32(1), bits - 2 - i)
        return jnp.where(count_ge(cand) >= n_sel, cand, t)

    return lax.fori_loop(0, bits - 1, bit_body, t)


def _kth_largest_key(count_ge, n_sel, shape):
    return _kth_largest(count_ge, n_sel, shape, 32)


def _dsa_kernel(q_ref, iq_ref, iw_ref, k_ref, v_ref, ik_ref, bias_ref, o_ref,
                s_ref, mb_ref, hi_ref, lo_ref, mx_ref, ls_ref, acc_ref, *, tq, n_sel):
    qi = pl.program_id(1)
    n = pl.program_id(2)

    def keys_at(off):
        return lax.bitcast_convert_type(s_ref[0:tq, pl.ds(off, tq)], I32)

    @pl.when(n == 0)
    def _():
        row = lax.broadcasted_iota(I32, (tq, tq), 0)
        colm = lax.broadcasted_iota(I32, (tq, tq), 1)

        def score_body(j, c):
            off = pl.multiple_of(j * tq, tq)
            ikb = ik_ref[pl.ds(off, tq), :]
            acc = jnp.zeros((tq, tq), F32)
            for h in range(IDX_HEADS):
                d = lax.dot_general(iq_ref[:, h * IDX_HD:(h + 1) * IDX_HD], ikb, _NT,
                                    preferred_element_type=F32)
                acc = acc + jnp.maximum(d, 0.0) * iw_ref[:, h:h + 1]
            lim = jnp.where(j < qi, tq, 0)
            acc = jnp.where(colm <= row + lim, acc, -jnp.inf)
            key = _sortable_key(acc)
            s_ref[0:tq, pl.ds(off, tq)] = lax.bitcast_convert_type(key, F32)
            hi_ref[:, pl.ds(off, tq)] = (key >> 16).astype(I16)
            lo_ref[:, pl.ds(off, tq)] = ((key & 0xFFFF) - (1 << 15)).astype(I16)
            return c

        lax.fori_loop(0, qi + 1, score_body, 0)

        def count16(plane_ref, pred):
            def col_body(j, pc):
                blk = plane_ref[:, pl.ds(pl.multiple_of(j * tq, tq), tq)]
                ind = jnp.where(pred(blk), jnp.int16(1), jnp.int16(0))
                return pc + _lane_fold(ind, jnp.add)
            pc = lax.fori_loop(0, qi + 1, col_body, jnp.zeros((tq, LANES), I16))
            return jnp.sum(pc.astype(I32), axis=1, keepdims=True)

        t_hi = _kth_largest(lambda c: count16(hi_ref, lambda b: b >= c.astype(I16)), n_sel, (tq, 1), 16)
        t_hi16 = t_hi.astype(I16)
        need = n_sel - count16(hi_ref, lambda b: b > t_hi16)

        def tie_body(j, c):
            sl = pl.ds(pl.multiple_of(j * tq, tq), tq)
            lo_ref[:, sl] = jnp.where(hi_ref[:, sl] == t_hi16, lo_ref[:, sl], jnp.int16(-(1 << 15)))
            return c

        lax.fori_loop(0, qi + 1, tie_body, 0)
        t_lo = _kth_largest(lambda c: count16(lo_ref, lambda b: b >= c.astype(I16)), need, (tq, 1), 16)
        thr = (t_hi << 16) | (t_lo + (1 << 15))

        def mask_body(j, c):
            off = pl.multiple_of(j * tq, tq)
            mb_ref[:, pl.ds(off, tq)] = jnp.where(keys_at(off) >= thr, 0.0, NEG)
            return c

        lax.fori_loop(0, qi + 1, mask_body, 0)

    hp = DSA_HEADS_PER_PASS
    for g0 in range(0, B_GROUP, hp):
        gs = range(g0, g0 + hp)
        q = jnp.concatenate([q_ref[:, g * B_HD:(g + 1) * B_HD] for g in gs], axis=0)

        def far_bias(off, n):
            mb = mb_ref[:, pl.ds(off, n)]
            return jnp.concatenate([mb] * hp, axis=0)

        def near_bias(delta, off, gs=gs):
            mb = mb_ref[:, pl.ds(off, tq)]
            return jnp.concatenate([mb + bias_ref[g, delta * tq:(delta + 1) * tq, :] for g in gs], axis=0)

        o = _attend([(q, lambda off, n: k_ref[pl.ds(off, n), :], near_bias, far_bias)],
                    lambda off, n: v_ref[pl.ds(off, n), :], s_ref, mx_ref, ls_ref, acc_ref, qi, tq)
        for i, g in enumerate(gs):
            o_ref[:, g * B_HD:(g + 1) * B_HD] = o[i * tq:(i + 1) * tq].astype(o_ref.dtype)


def _dsa_prompt(q, iq, iw, kb, vb, ikb, bias, *, batch, t_pad, n_sel):
    tq = ATT_TILE
    hp = DSA_HEADS_PER_PASS
    nq = t_pad // tq
    gw = B_GROUP * B_HD
    return pl.pallas_call(
        functools.partial(_dsa_kernel, tq=tq, n_sel=n_sel),
        grid=(batch, nq, B_KV_HEADS),
        in_specs=[pl.BlockSpec((tq, gw), lambda b, i, n: (b * nq + i, n)),
                  pl.BlockSpec((tq, I_Q), lambda b, i, n: (b * nq + i, 0)),
                  pl.BlockSpec((tq, LANES), lambda b, i, n: (b * nq + i, 0)),
                  pl.BlockSpec((t_pad, B_HD), lambda b, i, n: (b, n)),
                  pl.BlockSpec((t_pad, B_HD), lambda b, i, n: (b, n)),
                  pl.BlockSpec((t_pad, IDX_HD), lambda b, i, n: (b, 0)),
                  pl.BlockSpec((B_GROUP, 2 * tq, tq), lambda b, i, n: (n, 0, 0))],
        out_specs=pl.BlockSpec((tq, gw), lambda b, i, n: (b * nq + i, n)),
        out_shape=jax.ShapeDtypeStruct((batch * t_pad, B_Q), BF16),
        scratch_shapes=[pltpu.VMEM((hp * tq, t_pad), F32), pltpu.VMEM((tq, t_pad), F32),
                        pltpu.VMEM((tq, t_pad), I16), pltpu.VMEM((tq, t_pad), I16),
                        pltpu.VMEM((hp * tq, LANES), F32), pltpu.VMEM((hp * tq, LANES), F32),
                        pltpu.VMEM((hp * tq, B_HD), F32)],
        compiler_params=_cparams(("parallel", "arbitrary", "arbitrary"), DSA_VMEM_LIMIT_BYTES),
        name="dsa_prompt",
    )(q, iq, iw, kb, vb, ikb, bias)


def _page_specs(comps, layer, n_pages, npp):
    def spec(i):
        return pl.BlockSpec((1, 1, PAGE_SIZE * comps, LANES),
                            lambda b, p, pt, i=i: (layer, pt[b * n_pages + p * npp + i], 0, 0))
    return [spec(i) for i in range(npp)]


def _component(ref, c, comps):
    return ref[0, 0, pl.ds(c, PAGE_SIZE, stride=comps), :].astype(BF16)


def _sample_attn_body(p, npg, npp, qbd, k_refs, v_refs, kcomps, vcomps, v_ids, knew_ref, vnew_ref,
                      blast_ref, bnew_ref, m_ref, l_ref, acc_ref, mask_of):
    groups = len(v_ids)
    rg = LANES // groups
    vw = acc_ref.shape[1]

    @pl.when(p == 0)
    def _():
        m_ref[...] = jnp.full(m_ref.shape, NEG, F32)
        l_ref[...] = jnp.zeros_like(l_ref)
        acc_ref[...] = jnp.zeros_like(acc_ref)

    def scores(xcat, bias, mask_idx):
        s = lax.dot_general(qbd, xcat, _NT, preferred_element_type=F32)
        if bias is not None:
            s = s + bias
        return mask_of(mask_idx, s)

    def update(s_blocks, v_groups):
        s = jnp.concatenate(s_blocks, axis=1) if len(s_blocks) > 1 else s_blocks[0]
        m_old = m_ref[...]
        m_new = jnp.maximum(m_old, jnp.max(s, axis=1, keepdims=True))
        alpha = jnp.exp2(m_old - m_new)
        pr = jnp.exp2(s - m_new)
        l_ref[...] = alpha * l_ref[...] + jnp.sum(pr, axis=1, keepdims=True)
        m_ref[...] = m_new
        pb = pr.astype(BF16)
        for g in range(groups):
            rows = slice(g * rg, (g + 1) * rg)
            pv = None
            for i, v_group in enumerate(v_groups):
                d = jnp.dot(pb[rows, i * PAGE_SIZE:(i + 1) * PAGE_SIZE], v_group(g),
                            preferred_element_type=F32)
                pv = d if pv is None else pv + d
            acc_ref[rows, :] = alpha[rows] * acc_ref[rows, :] + pv

    is_last = p == npg - 1
    s_blocks, v_groups = [], []
    for i in range(npp):
        xcat = jnp.concatenate([_component(k_refs[i], c, kcomps) for c in range(kcomps)], axis=1)

        def v_group(g, i=i):
            parts = [_component(v_refs[i], c, vcomps) for c in v_ids[g]]
            return parts[0] if len(parts) == 1 else jnp.concatenate(parts, axis=1)

        bias = jnp.where(is_last, 1.0, 0.0) * blast_ref[...] if i == npp - 1 else None
        s_blocks.append(scores(xcat, bias, i))
        v_groups.append(v_group)
    update(s_blocks, v_groups)

    @pl.when(is_last)
    def _():
        update([scores(knew_ref[0], bnew_ref[...], npp)],
               [lambda g: vnew_ref[0, :, g * vw:(g + 1) * vw]])


def _diff_sample_kernel(pt_ref, q_ref, *refs, npp, npg, lam_init, n_new):
    k_refs = refs[:npp]
    v_refs = refs[npp:2 * npp]
    (knew_ref, vnew_ref, blast_ref, bnew_ref, lamp_ref, sub_ref,
     o_ref, m_ref, l_ref, acc_ref) = refs[2 * npp:]
    p = pl.program_id(1)
    comps = 2 * A_HEADS
    v_ids = [(h, A_HEADS + h) for h in range(A_HEADS)]
    _sample_attn_body(p, npg, npp, q_ref[0], k_refs, v_refs, comps, comps, v_ids, knew_ref, vnew_ref,
                      blast_ref, bnew_ref, m_ref, l_ref, acc_ref, lambda i, s: s)

    @pl.when(p == npg - 1)
    def _():
        lam = _diff_lambda(lamp_ref[...], lam_init)
        o = acc_ref[...] / l_ref[...]
        hw = 2 * A_HD
        for h in range(A_HEADS):
            r = 2 * h * SUBLANES
            od = o[r:r + SUBLANES] - lam * o[r + SUBLANES:r + 2 * SUBLANES]
            od = _head_rms(od, sub_ref[...]) * (1.0 - lam_init)
            o_ref[0, :, h * hw:(h + 1) * hw] = od[:n_new]


def _diff_attn_sample(qbd, cache_k, cache_v, pt_flat, knew, vnew, blast, bnew, lam_p, subln,
                      *, layer, n_pages, lam_init, n_new):
    db = qbd.shape[0]
    npp = _pages_per_step(n_pages, DIFF_SAMPLE_PAGES)
    npg = n_pages // npp
    hw = 2 * A_HD
    comps = 2 * A_HEADS
    per_b = lambda shape: pl.BlockSpec((1,) + shape, lambda b, p, pt: (b, 0, 0))
    const = lambda shape: pl.BlockSpec(shape, lambda b, p, pt: (0, 0))
    grid_spec = pltpu.PrefetchScalarGridSpec(
        num_scalar_prefetch=1,
        grid=(db, npg),
        in_specs=([per_b((LANES, A_QK))] + _page_specs(comps, layer, n_pages, npp)
                  + _page_specs(comps, layer, n_pages, npp)
                  + [per_b((PAGE_SIZE, A_QK)), per_b((PAGE_SIZE, A_QK)),
                     const((LANES, PAGE_SIZE)), const((LANES, PAGE_SIZE)),
                     const((4, A_HD)), const((1, hw))]),
        out_specs=pl.BlockSpec((1, n_new, A_QK), lambda b, p, pt: (b, 0, 0)),
        scratch_shapes=[pltpu.VMEM((LANES, 1), F32), pltpu.VMEM((LANES, 1), F32),
                        pltpu.VMEM((LANES, hw), F32)],
    )
    return pl.pallas_call(
        functools.partial(_diff_sample_kernel, npp=npp, npg=npg, lam_init=lam_init, n_new=n_new),
        grid_spec=grid_spec,
        out_shape=jax.ShapeDtypeStruct((db, n_new, A_QK), F32),
        compiler_params=_cparams(("parallel", "arbitrary")),
        name="diff_attn_sample",
    )(pt_flat, qbd, *([cache_k] * npp), *([cache_v] * npp), knew, vnew, blast, bnew,
      lam_p.astype(F32), subln.reshape(1, hw).astype(F32))


def _idx_sample_kernel(pt_ref, iq_ref, w_ref, *refs, npp, npg, n_pages, n_sel, n_new):
    ik_refs = refs[:npp]
    iknew_ref, keys_ref, thr_ref = refs[npp:]
    p = pl.program_id(1)
    iq = iq_ref[0]
    w = w_ref[0]

    def slot_scores(ik):
        d = lax.dot_general(iq, ik, _NT, preferred_element_type=F32)
        r = jnp.maximum(d, 0.0) * w
        return jnp.sum(r.reshape(SUBLANES, IDX_HEADS, PAGE_SIZE), axis=1)

    for i in range(npp):
        sc = slot_scores(ik_refs[i][0, 0].astype(BF16))
        off = pl.multiple_of((p * npp + i) * PAGE_SIZE, PAGE_SIZE)
        keys_ref[0, :, pl.ds(off, PAGE_SIZE)] = _sortable_key(sc)

    @pl.when(p == npg - 1)
    def _():
        sc = slot_scores(iknew_ref[0])
        srow = lax.broadcasted_iota(I32, sc.shape, 0)
        tcol = lax.broadcasted_iota(I32, sc.shape, 1)
        sc = jnp.where((tcol <= srow) & (tcol < n_new), sc, -jnp.inf)
        keys_ref[0, :, n_pages * PAGE_SIZE:(n_pages + 1) * PAGE_SIZE] = _sortable_key(sc)

        def count_ge(cand):
            return jnp.sum(jnp.where(keys_ref[0] >= cand, 1, 0), axis=1, keepdims=True)

        t = _kth_largest_key(count_ge, n_sel, (SUBLANES, 1))
        thr_ref[0] = jnp.broadcast_to(t, (SUBLANES, LANES))


def _idx_sample(iqs, ws, cache_ki, pt_flat, iknew, *, layer, n_pages, n_sel, n_new):
    db = iqs.shape[0]
    npp = _pages_per_step(n_pages, IDX_SAMPLE_PAGES)
    npg = n_pages // npp
    width = (n_pages + 1) * PAGE_SIZE
    per_b = lambda shape: pl.BlockSpec((1,) + shape, lambda b, p, pt: (b, 0, 0))
    grid_spec = pltpu.PrefetchScalarGridSpec(
        num_scalar_prefetch=1,
        grid=(db, npg),
        in_specs=([per_b((SUBLANES * IDX_HEADS, IDX_HD)), per_b((SUBLANES * IDX_HEADS, 1))]
                  + _page_specs(1, layer, n_pages, npp) + [per_b((PAGE_SIZE, IDX_HD))]),
        out_specs=[per_b((SUBLANES, width)), per_b((SUBLANES, LANES))],
    )
    return pl.pallas_call(
        functools.partial(_idx_sample_kernel, npp=npp, npg=npg, n_pages=n_pages, n_sel=n_sel, n_new=n_new),
        grid_spec=grid_spec,
        out_shape=[jax.ShapeDtypeStruct((db, SUBLANES, width), I32),
                   jax.ShapeDtypeStruct((db, SUBLANES, LANES), I32)],
        compiler_params=_cparams(("parallel", "arbitrary")),
        name="dsa_index_sample",
    )(pt_flat, iqs, ws, *([cache_ki] * npp), iknew)


def _dsa_sample_kernel(pt_ref, q_ref, *refs, npp, npg, n_new):
    k_refs = refs[:npp]
    v_refs = refs[npp:2 * npp]
    key_refs = refs[2 * npp:3 * npp + 1]
    (thr_ref, knew_ref, vnew_ref, blast_ref, bnew_ref,
     o_ref, m_ref, l_ref, acc_ref) = refs[3 * npp + 1:]
    p = pl.program_id(1)

    def mask_of(i, s):
        keys = jnp.concatenate([key_refs[i][0]] * B_HEADS, axis=0)
        thr = jnp.concatenate([thr_ref[0]] * B_HEADS, axis=0)
        return jnp.where(keys >= thr, s, NEG)

    v_ids = [(n,) for n in range(B_KV_HEADS)]
    _sample_attn_body(p, npg, npp, q_ref[0], k_refs, v_refs, B_KV_HEADS, B_KV_HEADS, v_ids, knew_ref,
                      vnew_ref, blast_ref, bnew_ref, m_ref, l_ref, acc_ref, mask_of)

    @pl.when(p == npg - 1)
    def _():
        o = acc_ref[...] / l_ref[...]
        for hd in range(B_HEADS):
            o_ref[0, :, hd * B_HD:(hd + 1) * B_HD] = o[hd * SUBLANES:hd * SUBLANES + n_new]


def _dsa_attn_sample(qbd, cache_k, cache_v, keys, thr, pt_flat, knew, vnew, blast, bnew,
                     *, layer, n_pages, n_new):
    db = qbd.shape[0]
    npp = _pages_per_step(n_pages, DSA_SAMPLE_PAGES)
    npg = n_pages // npp
    per_b = lambda shape: pl.BlockSpec((1,) + shape, lambda b, p, pt: (b, 0, 0))
    const = lambda shape: pl.BlockSpec(shape, lambda b, p, pt: (0, 0))
    key_specs = [pl.BlockSpec((1, SUBLANES, PAGE_SIZE), lambda b, p, pt, i=i: (b, 0, p * npp + i))
                 for i in range(npp)]
    key_specs.append(pl.BlockSpec((1, SUBLANES, PAGE_SIZE), lambda b, p, pt: (b, 0, n_pages)))
    grid_spec = pltpu.PrefetchScalarGridSpec(
        num_scalar_prefetch=1,
        grid=(db, npg),
        in_specs=([per_b((LANES, B_KV))] + _page_specs(B_KV_HEADS, layer, n_pages, npp)
                  + _page_specs(B_KV_HEADS, layer, n_pages, npp) + key_specs
                  + [per_b((SUBLANES, LANES)), per_b((PAGE_SIZE, B_KV)), per_b((PAGE_SIZE, B_KV)),
                     const((LANES, PAGE_SIZE)), const((LANES, PAGE_SIZE))]),
        out_specs=pl.BlockSpec((1, n_new, B_Q), lambda b, p, pt: (b, 0, 0)),
        scratch_shapes=[pltpu.VMEM((LANES, 1), F32), pltpu.VMEM((LANES, 1), F32),
                        pltpu.VMEM((LANES, B_HD), F32)],
    )
    return pl.pallas_call(
        functools.partial(_dsa_sample_kernel, npp=npp, npg=npg, n_new=n_new),
        grid_spec=grid_spec,
        out_shape=jax.ShapeDtypeStruct((db, n_new, B_Q), F32),
        compiler_params=_cparams(("parallel", "arbitrary")),
        name="dsa_attn_sample",
    )(pt_flat, qbd, *([cache_k] * npp), *([cache_v] * npp), *([keys] * (npp + 1)), thr, knew, vnew,
      blast, bnew)


def _pad_rows(x, rows):
    return jnp.pad(x, ((0, 0), (0, rows - x.shape[1]), (0, 0)))


def kernel(x_prompt, x_sample, cache_a_k, cache_a_v, cache_b_k, cache_b_v, cache_b_kidx, page_table, meta_tokens, rel_bias_table, norm_mix, norm_ffn, a_w_in, a_w_out, a_q_norm, a_k_norm, a_lambda, a_subln, b_w_in, b_w_out, b_q_norm, b_k_norm, ffn_w1, ffn_w2):
    batch, seq, d_model = x_prompt.shape
    db, ds, _ = x_sample.shape
    depth = norm_mix.shape[0]
    n_pages = page_table.shape[1]
    past_len = n_pages * PAGE_SIZE
    t_real = seq + N_META
    t_pad = -(-t_real // Q_BLOCK) * Q_BLOCK
    tq = ATT_TILE
    n_s = db * ds
    r0 = batch * t_pad
    m_tot = r0 + TAIL_ROWS
    assert t_pad % tq == 0 and r0 % TAIL_ROWS == 0 and n_s <= TAIL_ROWS and ds <= SUBLANES
    n_sel_prompt = min(TOPK_MAX, seq // 4)
    n_sel_sample = min(TOPK_MAX, (past_len + ds) // 4)

    meta = jnp.broadcast_to(meta_tokens.astype(F32), (batch, N_META, d_model))
    xp = jnp.concatenate([meta, x_prompt, jnp.zeros((batch, t_pad - t_real, d_model), F32)], axis=1)
    x = jnp.concatenate([xp.reshape(r0, d_model), x_sample.reshape(n_s, d_model),
                         jnp.zeros((TAIL_ROWS - n_s, d_model), F32)], axis=0)

    ri = lax.broadcasted_iota(I32, (tq, tq), 0)
    ci = lax.broadcasted_iota(I32, (tq, tq), 1)
    r8 = lax.broadcasted_iota(I32, (SUBLANES, tq), 0)
    c8 = lax.broadcasted_iota(I32, (SUBLANES, tq), 1)
    dist = jnp.concatenate([ri - ci, tq + ri - ci, PAGE_SIZE + r8 - c8, r8 - c8], axis=0)
    bias_all = _bias_tiles(rel_bias_table, dist)
    bias_prompt = bias_all[:, :2 * tq]
    b_last = bias_all[:, 2 * tq:2 * tq + SUBLANES, :PAGE_SIZE].reshape(N_BIAS_HEADS * SUBLANES, PAGE_SIZE)
    b_new = bias_all[:, 2 * tq + SUBLANES:2 * tq + 2 * SUBLANES, :PAGE_SIZE].reshape(
        N_BIAS_HEADS * SUBLANES, PAGE_SIZE)

    pt_flat = page_table.reshape(-1).astype(I32)
    n_a, n_pool = cache_a_k.shape[:2]
    n_b = cache_b_k.shape[0]
    cache_a_k2 = cache_a_k.reshape(n_a, n_pool, PAGE_SIZE * 2 * A_HEADS, A_HD)
    cache_a_v2 = cache_a_v.reshape(n_a, n_pool, PAGE_SIZE, A_HEADS, 2, A_HD).transpose(
        0, 1, 2, 4, 3, 5).reshape(n_a, n_pool, PAGE_SIZE * 2 * A_HEADS, A_HD)
    cache_b_k2 = cache_b_k.reshape(n_b, n_pool, PAGE_SIZE * B_KV_HEADS, B_HD)
    cache_b_v2 = cache_b_v.reshape(n_b, n_pool, PAGE_SIZE * B_KV_HEADS, B_HD)

    b_in_pad = -(-B_IN // 256) * 256
    wa_in, wa_out = a_w_in.astype(BF16), a_w_out.astype(BF16)
    wb_in = jnp.pad(b_w_in, ((0, 0), (0, 0), (0, b_in_pad - B_IN))).astype(BF16)
    wb_out, w_ffn1, w_ffn2 = b_w_out.astype(BF16), ffn_w1.astype(BF16), ffn_w2.astype(BF16)
    eye_m = jnp.eye(2 * A_HEADS, dtype=BF16)
    eye_n = jnp.eye(B_KV_HEADS, dtype=BF16)

    def rows_of(a):
        return a[:n_s].reshape((db, ds) + a.shape[1:])

    n_a_layers = (depth + N_MIXERS - 1) // N_MIXERS
    n_b_layers = depth // N_MIXERS
    a_states, b_states = (), ()
    outs = {k: [] for k in ("aks", "avs", "bks", "bvs", "bis")}
    for i in range(depth):
        j = i // N_MIXERS
        h = _rmsnorm(x, norm_mix[i])
        if i % N_MIXERS == 0:
            lam_init = 0.8 - 0.6 * math.exp(-0.3 * i)
            z = _matmul(h, wa_in, j)
            dims = (batch, t_pad, t_real, r0, n_a_layers, j)
            q, k_st, kb, vb, v_st = _a_post(z, a_q_norm[j], a_k_norm[j], "prompt", dims, a_states)
            a_states = (k_st, v_st)
            if t_real % POST_TILE:
                _, k_st, _, _, v_st = _a_post(z, a_q_norm[j], a_k_norm[j], "edge", dims, a_states)
                a_states = (k_st, v_st)
            q_t, k_t, kb_t, vb_t, v_t = _a_post(z, a_q_norm[j], a_k_norm[j], "tail", dims)
            o_p = _diff_attn_prompt(q, kb, vb, bias_prompt, a_lambda[j], a_subln[j],
                                    batch=batch, t_pad=t_pad, lam_init=lam_init)
            qs = _pad_rows(rows_of(q_t), SUBLANES).reshape(db, SUBLANES, 2 * A_HEADS, A_HD)
            qbd = (qs[:, None] * eye_m[None, :, None, :, None]).reshape(db, LANES, A_QK)
            o_s = _diff_attn_sample(qbd, cache_a_k2, cache_a_v2, pt_flat,
                                    _pad_rows(rows_of(kb_t), PAGE_SIZE), _pad_rows(rows_of(vb_t), PAGE_SIZE),
                                    b_last, b_new, a_lambda[j], a_subln[j],
                                    layer=j, n_pages=n_pages, lam_init=lam_init, n_new=ds)
            w_out = wa_out
            outs["aks"].append(rows_of(k_t.reshape(TAIL_ROWS, A_HEADS, 2, A_HD)))
            outs["avs"].append(rows_of(v_t.reshape(TAIL_ROWS, 2, A_HEADS, A_HD).transpose(0, 2, 1, 3)
                                       .reshape(TAIL_ROWS, A_HEADS, 2 * A_HD)))
        else:
            z = _matmul(h, wb_in, j)
            dims = (batch, t_pad, t_real, r0, n_b_layers, j)
            q, k_st, kb, vb, iq, ikb, iw, v_st, ik_st = _b_post(z, b_q_norm[j], b_k_norm[j], "prompt", dims,
                                                                b_states)
            b_states = (k_st, v_st, ik_st)
            if t_real % POST_TILE:
                edge = _b_post(z, b_q_norm[j], b_k_norm[j], "edge", dims, b_states)
                b_states = (edge[1], edge[7], edge[8])
            q_t, k_t, kb_t, vb_t, iq_t, ikb_t, iw_t, v_t, ik_t = _b_post(z, b_q_norm[j], b_k_norm[j], "tail", dims)
            o_p = _dsa_prompt(q, iq, iw, kb, vb, ikb, bias_prompt,
                              batch=batch, t_pad=t_pad, n_sel=n_sel_prompt)
            iqs = _pad_rows(rows_of(iq_t), SUBLANES).reshape(db, SUBLANES * IDX_HEADS, IDX_HD)
            ws = _pad_rows(rows_of(iw_t)[:, :, :IDX_HEADS], SUBLANES).reshape(db, SUBLANES * IDX_HEADS, 1)
            keys, thr = _idx_sample(iqs, ws, cache_b_kidx, pt_flat, _pad_rows(rows_of(ikb_t), PAGE_SIZE),
                                    layer=j, n_pages=n_pages, n_sel=n_sel_sample, n_new=ds)
            qs = _pad_rows(rows_of(q_t), SUBLANES).reshape(db, SUBLANES, B_KV_HEADS, B_GROUP, B_HD)
            qbd = (qs.transpose(0, 3, 1, 2, 4)[:, None]
                   * eye_n[None, :, None, None, :, None]).reshape(db, LANES, B_KV)
            o_s = _dsa_attn_sample(qbd, cache_b_k2, cache_b_v2, keys, thr, pt_flat,
                                   _pad_rows(rows_of(kb_t), PAGE_SIZE), _pad_rows(rows_of(vb_t), PAGE_SIZE),
                                   b_last, b_new, layer=j, n_pages=n_pages, n_new=ds)
            w_out = wb_out
            outs["bks"].append(rows_of(k_t.reshape(TAIL_ROWS, B_KV_HEADS, B_HD)))
            outs["bvs"].append(rows_of(v_t.reshape(TAIL_ROWS, B_KV_HEADS, B_HD)))
            outs["bis"].append(rows_of(ik_t))
        width = o_p.shape[1]
        o = jnp.concatenate([o_p, o_s.reshape(n_s, width).astype(BF16),
                             jnp.zeros((TAIL_ROWS - n_s, width), BF16)], axis=0)
        x = _matmul(o, w_out, j, mode="res", res=x)
        hf = _rmsnorm(x, norm_ffn[i])
        u = _matmul(hf, w_ffn1, i, mode="relu2", out_dtype=BF16)
        x = _matmul(u, w_ffn2, i, mode="res", res=x)

    y_prompt = x[:r0].reshape(batch, t_pad, d_model)[:, N_META:t_real]
    y_sample = x[r0:r0 + n_s].reshape(db, ds, d_model)
    st = lambda key: jnp.stack(outs[key])
    a_k_p = a_states[0].reshape(n_a_layers, batch, t_real, A_HEADS, 2, A_HD)
    a_v_p = a_states[1].reshape(n_a_layers, batch, t_real, 2, A_HEADS, A_HD).transpose(
        0, 1, 2, 4, 3, 5).reshape(n_a_layers, batch, t_real, A_HEADS, 2 * A_HD)
    b_k_p = b_states[0].reshape(n_b_layers, batch, t_real, B_KV_HEADS, B_HD)
    b_v_p = b_states[1].reshape(n_b_layers, batch, t_real, B_KV_HEADS, B_HD)
    return (y_prompt, y_sample, a_k_p, a_v_p, b_k_p, b_v_p, b_states[2],
            st("aks"), st("avs"), st("bks"), st("bvs"), st("bis"))
```

```python
import functools
import math

import jax
import jax.numpy as jnp
from jax import lax
from jax.experimental import pallas as pl
from jax.experimental.pallas import tpu as pltpu

F32 = jnp.float32
BF16 = jnp.bfloat16
I32 = jnp.int32

N_META = 16
N_MIXERS = 2
A_HEADS = 8
A_HD = 128
A_QK = A_HEADS * 2 * A_HD
B_HEADS = 16
B_KV_HEADS = 4
B_GROUP = B_HEADS // B_KV_HEADS
B_HD = 128
B_Q = B_HEADS * B_HD
B_KV = B_KV_HEADS * B_HD
IDX_HEADS = 16
IDX_HD = 128
I_Q = IDX_HEADS * IDX_HD
B_IN = B_Q + 2 * B_KV + I_Q + IDX_HD + IDX_HEADS
TOPK_MAX = 256
N_BUCKETS = 32
MAX_DISTANCE = 128
N_BIAS_HEADS = 16
PAGE_SIZE = 128
Q_BLOCK = 128
EPS = 1e-6

LANES = 128
SUBLANES = 8
ATT_TILE = 384
RESIDENT_MAX_K = 2048
POST_TILE = 128
TAIL_ROWS = 256
NEG = -1e30
INT_MIN = -2 ** 31
LOG2E = math.log2(math.e)
DIFF_SAMPLE_PAGES = 4
DSA_SAMPLE_PAGES = 8
IDX_SAMPLE_PAGES = 16
CHUNKS_PER_TRIP = 4
DSA_HEADS_PER_PASS = 2
VMEM_LIMIT_BYTES = 48 * 1024 * 1024
DSA_VMEM_LIMIT_BYTES = 58 * 1024 * 1024
I16 = jnp.int16

_NT = (((1,), (1,)), ((), ()))


def _cparams(sem, vmem_limit=VMEM_LIMIT_BYTES):
    return pltpu.CompilerParams(dimension_semantics=sem, vmem_limit_bytes=vmem_limit)


def _pages_per_step(n_pages, want):
    return max(c for c in range(1, want + 1) if n_pages % c == 0)


def _pick(n, candidates):
    for c in candidates:
        if n % c == 0:
            return c
    raise ValueError(f"no tile for {n} in {candidates}")


def _rms_kernel(x_ref, g_ref, o_ref):
    x = x_ref[...]
    ms = jnp.mean(x * x, axis=-1, keepdims=True)
    o_ref[...] = (x * lax.rsqrt(ms + EPS) * g_ref[...]).astype(o_ref.dtype)


def _rmsnorm(x, g):
    m, d = x.shape
    tm = _pick(m, (512, 256))
    return pl.pallas_call(
        _rms_kernel,
        grid=(m // tm,),
        in_specs=[pl.BlockSpec((tm, d), lambda i: (i, 0)),
                  pl.BlockSpec((1, d), lambda i: (0, 0))],
        out_specs=pl.BlockSpec((tm, d), lambda i: (i, 0)),
        out_shape=jax.ShapeDtypeStruct((m, d), BF16),
        compiler_params=_cparams(("parallel",)),
        name="rmsnorm",
    )(x, g.reshape(1, d).astype(F32))


def _mm_kernel(*refs, nk, mode):
    if mode == "res":
        a_ref, w_ref, r_ref, o_ref, acc_ref = refs
    else:
        a_ref, w_ref, o_ref, acc_ref = refs
        r_ref = None
    k = pl.program_id(2)

    @pl.when(k == 0)
    def _():
        acc_ref[...] = jnp.zeros_like(acc_ref)

    acc_ref[...] += jnp.dot(a_ref[...], w_ref[...], preferred_element_type=F32)

    @pl.when(k == nk - 1)
    def _():
        acc = acc_ref[...]
        if mode == "relu2":
            r = jnp.maximum(acc, 0.0)
            acc = r * r
        elif mode == "res":
            acc = acc + r_ref[...]
        o_ref[...] = acc.astype(o_ref.dtype)


def _mm_resident_kernel(a_ref, w_ref, *rest, mode):
    if mode == "res":
        r_ref, o_ref, wb_ref = rest
    else:
        o_ref, wb_ref = rest

    @pl.when(pl.program_id(1) == 0)
    def _():
        wb_ref[...] = w_ref[...].astype(BF16)

    acc = jnp.dot(a_ref[...], wb_ref[...], preferred_element_type=F32)
    if mode == "relu2":
        r = jnp.maximum(acc, 0.0)
        acc = r * r
    elif mode == "res":
        acc = acc + r_ref[...]
    o_ref[...] = acc.astype(o_ref.dtype)


def _matmul_resident(a, w, layer, *, mode="plain", res=None, out_dtype=F32):
    m, kd = a.shape
    n = w.shape[2]
    tm = _pick(m, (1088, 512, 256))
    tn = _pick(n, (512, 256, 128) if mode == "res" else (1024, 512, 256, 128))
    in_specs = [pl.BlockSpec((tm, kd), lambda j, i: (i, 0)),
                pl.BlockSpec((None, kd, tn), lambda j, i: (layer, 0, j))]
    args = [a, w]
    if mode == "res":
        in_specs.append(pl.BlockSpec((tm, tn), lambda j, i: (i, j)))
        args.append(res)
    return pl.pallas_call(
        functools.partial(_mm_resident_kernel, mode=mode),
        grid=(n // tn, m // tm),
        in_specs=in_specs,
        out_specs=pl.BlockSpec((tm, tn), lambda j, i: (i, j)),
        out_shape=jax.ShapeDtypeStruct((m, n), out_dtype),
        scratch_shapes=[pltpu.VMEM((kd, tn), BF16)],
        compiler_params=_cparams(("parallel", "arbitrary")),
        name="matmul_resident_" + mode,
    )(*args)


def _matmul(a, w, layer, *, mode="plain", res=None, out_dtype=F32):
    m, kd = a.shape
    n = w.shape[2]
    tm = _pick(m, (1088, 512, 256))
    tn = _pick(n, (1024, 768, 512, 256, 128))
    tk = _pick(kd, (2048, 1024, 512, 256))
    nk = kd // tk
    in_specs = [pl.BlockSpec((tm, tk), lambda i, j, k: (i, k)),
                pl.BlockSpec((None, tk, tn), lambda i, j, k: (layer, k, j))]
    args = [a, w]
    if mode == "res":
        in_specs.append(pl.BlockSpec((tm, tn), lambda i, j, k: (i, j)))
        args.append(res)
    return pl.pallas_call(
        functools.partial(_mm_kernel, nk=nk, mode=mode),
        grid=(m // tm, n // tn, nk),
        in_specs=in_specs,
        out_specs=pl.BlockSpec((tm, tn), lambda i, j, k: (i, j)),
        out_shape=jax.ShapeDtypeStruct((m, n), out_dtype),
        scratch_shapes=[pltpu.VMEM((tm, tn), F32)],
        compiler_params=_cparams(("parallel", "parallel", "arbitrary")),
        name="matmul_" + mode,
    )(*args)


def _bias_kernel(tab_ref, d_ref, o_ref):
    h = pl.program_id(0)
    d = d_ref[...]
    max_exact = N_BUCKETS // 2
    n = jnp.maximum(d, 0)
    nf = jnp.maximum(n, 1).astype(F32)
    large = max_exact + (jnp.log(nf / max_exact) / math.log(MAX_DISTANCE / max_exact)
                         * (N_BUCKETS - max_exact)).astype(I32)
    bucket = jnp.where(n < max_exact, n, jnp.minimum(large, N_BUCKETS - 1))
    acc = jnp.zeros(d.shape, F32)
    for b in range(N_BUCKETS):
        acc = jnp.where(bucket == b, tab_ref[b * N_BIAS_HEADS + h], acc)
    far = tab_ref[(N_BUCKETS - 1) * N_BIAS_HEADS + h]
    o_ref[0] = jnp.where(d < 0, NEG, (acc - far) * LOG2E)


def _bias_tiles(table, dist):
    r, c = dist.shape
    return pl.pallas_call(
        _bias_kernel,
        grid=(N_BIAS_HEADS,),
        in_specs=[pl.BlockSpec(memory_space=pltpu.SMEM),
                  pl.BlockSpec((r, c), lambda h: (0, 0))],
        out_specs=pl.BlockSpec((1, r, c), lambda h: (h, 0, 0)),
        out_shape=jax.ShapeDtypeStruct((N_BIAS_HEADS, r, c), F32),
        compiler_params=_cparams(("arbitrary",)),
        name="bias_tiles",
    )(table.astype(F32).reshape(-1), dist)


def _head_rms(x, g):
    ms = jnp.mean(x * x, axis=-1, keepdims=True)
    return x * lax.rsqrt(ms + EPS) * g


def _post_specs(kind, batch, t_pad, t_real, r0, layers, layer):
    tm = POST_TILE
    tiles = t_pad // tm
    whole = t_real // tm
    rem = t_real - whole * tm
    state_rows, state_tiles = tm, None
    state_sds = lambda comps: jax.ShapeDtypeStruct((layers, batch, t_real * comps, LANES), F32)
    if kind == "prompt":
        grid = (batch, tiles)
        rows = lambda b, t: b * tiles + t
        n_rows = batch * t_pad
        state_tiles = whole
        state_spec = lambda comps: pl.BlockSpec((None, None, tm * comps, LANES),
                                                lambda b, t: (layer, b, jnp.minimum(t, whole - 1), 0))
    elif kind == "edge":
        assert rem % 16 == 0 and (whole * tm) % rem == 0
        grid = (batch, 1)
        rows = lambda b, t: b * tiles + whole
        n_rows = batch * tm
        state_rows = rem
        state_spec = lambda comps: pl.BlockSpec((None, None, rem * comps, LANES),
                                                lambda b, t: (layer, b, whole * tm // rem, 0))
    else:
        grid = (1, TAIL_ROWS // tm)
        rows = lambda b, t: r0 // tm + t
        n_rows = TAIL_ROWS
        state_spec = lambda comps: pl.BlockSpec((tm * comps, LANES), lambda b, t: (t, 0))
        state_sds = lambda comps: jax.ShapeDtypeStruct((TAIL_ROWS * comps, LANES), F32)
    in_blk = lambda w, c: pl.BlockSpec((tm, w), lambda b, t, c=c: (rows(b, t), c))
    out_blk = lambda w: pl.BlockSpec((tm, w), lambda b, t: (b * grid[1] + t, 0))
    out_sds = lambda w, dt: jax.ShapeDtypeStruct((n_rows, w), dt)
    return grid, in_blk, out_blk, out_sds, state_spec, state_sds, state_rows, state_tiles


def _when_state_tile(state_tiles, fn):
    if state_tiles is None:
        fn()
    else:
        pl.when(pl.program_id(1) < state_tiles)(fn)


def _a_post_kernel(zq_ref, zk_ref, zv_ref, qg_ref, kg_ref, *rest, state_rows, state_tiles):
    q_ref, k_ref, kb_ref, vb_ref, v_ref = rest[-5:]
    scale = A_HD ** -0.5 * LOG2E
    comps = 2 * A_HEADS
    for g in range(comps):
        sl = slice(g * A_HD, (g + 1) * A_HD)
        q_ref[:, sl] = (_head_rms(zq_ref[:, sl], qg_ref[...]) * scale).astype(BF16)
        kn = _head_rms(zk_ref[:, sl], kg_ref[...])
        kb_ref[:, sl] = kn.astype(BF16)
        head, half = g // 2, g % 2

        def store_state(g=g, sl=sl, kn=kn, c_v=half * A_HEADS + head):
            k_ref[pl.ds(g, state_rows, stride=comps), :] = kn[:state_rows]
            v_ref[pl.ds(c_v, state_rows, stride=comps), :] = zv_ref[:state_rows, sl]

        _when_state_tile(state_tiles, store_state)
    vb_ref[...] = zv_ref[...].astype(BF16)


def _a_post(z, q_gain, k_gain, kind, dims, states=()):
    comps = 2 * A_HEADS
    grid, in_blk, out_blk, out_sds, state_spec, state_sds, state_rows, state_tiles = _post_specs(kind, *dims)
    gain = pl.BlockSpec((1, A_HD), lambda b, t: (0, 0))
    return pl.pallas_call(
        functools.partial(_a_post_kernel, state_rows=state_rows, state_tiles=state_tiles),
        grid=grid,
        in_specs=([in_blk(A_QK, 0), in_blk(A_QK, 1), in_blk(A_QK, 2), gain, gain]
                  + [pl.BlockSpec(memory_space=pl.ANY)] * len(states)),
        out_specs=[out_blk(A_QK), state_spec(comps), out_blk(A_QK), out_blk(A_QK), state_spec(comps)],
        out_shape=[out_sds(A_QK, BF16), state_sds(comps), out_sds(A_QK, BF16), out_sds(A_QK, BF16),
                   state_sds(comps)],
        input_output_aliases=({5: 1, 6: 4} if states else {}),
        compiler_params=_cparams(("parallel", "parallel")),
        name="diff_post_" + kind,
    )(z, z, z, q_gain.reshape(1, A_HD).astype(F32), k_gain.reshape(1, A_HD).astype(F32), *states)


def _b_post_kernel(zq_ref, zk_ref, zv_ref, zi0_ref, zi1_ref, zik_ref, ziw_ref, qg_ref, kg_ref, *rest,
                   state_rows, state_tiles):
    q_ref, k_ref, kb_ref, vb_ref, iq_ref, ikb_ref, iw_ref, v_ref, ik_ref = rest[-9:]
    for g in range(B_HEADS):
        sl = slice(g * B_HD, (g + 1) * B_HD)
        q_ref[:, sl] = (_head_rms(zq_ref[:, sl], qg_ref[...]) * (B_HD ** -0.5 * LOG2E)).astype(BF16)
    for g in range(B_KV_HEADS):
        sl = slice(g * B_HD, (g + 1) * B_HD)
        kn = _head_rms(zk_ref[:, sl], kg_ref[...])
        kb_ref[:, sl] = kn.astype(BF16)

        def store_state(g=g, sl=sl, kn=kn):
            k_ref[pl.ds(g, state_rows, stride=B_KV_HEADS), :] = kn[:state_rows]
            v_ref[pl.ds(g, state_rows, stride=B_KV_HEADS), :] = zv_ref[:state_rows, sl]

        _when_state_tile(state_tiles, store_state)
    vb_ref[...] = zv_ref[...].astype(BF16)
    half = I_Q // 2
    iq_ref[:, :half] = (zi0_ref[...] * (IDX_HD ** -0.5)).astype(BF16)
    iq_ref[:, half:] = (zi1_ref[...] * (IDX_HD ** -0.5)).astype(BF16)

    def store_index_key():
        ik_ref[...] = zik_ref[:state_rows, :]

    _when_state_tile(state_tiles, store_index_key)
    ikb_ref[...] = zik_ref[...].astype(BF16)
    iw_ref[...] = ziw_ref[...] * (IDX_HEADS ** -0.5)


def _b_post(z, q_gain, k_gain, kind, dims, states=()):
    grid, blk, out, sds, state_spec, state_sds, state_rows, state_tiles = _post_specs(kind, *dims)
    gain = pl.BlockSpec((1, B_HD), lambda b, t: (0, 0))
    o1 = B_Q
    o2 = o1 + B_KV
    o3 = o2 + B_KV
    o4 = o3 + I_Q
    half = I_Q // 2
    kv = B_KV_HEADS
    return pl.pallas_call(
        functools.partial(_b_post_kernel, state_rows=state_rows, state_tiles=state_tiles),
        grid=grid,
        in_specs=([blk(B_Q, 0), blk(B_KV, o1 // B_KV), blk(B_KV, o2 // B_KV),
                   blk(half, o3 // half), blk(half, o3 // half + 1),
                   blk(IDX_HD, o4 // IDX_HD), blk(LANES, o4 // IDX_HD + 1), gain, gain]
                  + [pl.BlockSpec(memory_space=pl.ANY)] * len(states)),
        out_specs=[out(B_Q), state_spec(kv), out(B_KV), out(B_KV), out(I_Q), out(IDX_HD), out(LANES),
                   state_spec(kv), state_spec(1)],
        out_shape=[sds(B_Q, BF16), state_sds(kv), sds(B_KV, BF16), sds(B_KV, BF16),
                   sds(I_Q, BF16), sds(IDX_HD, BF16), sds(LANES, F32), state_sds(kv), state_sds(1)],
        input_output_aliases=({9: 1, 10: 7, 11: 8} if states else {}),
        compiler_params=_cparams(("parallel", "parallel")),
        name="dsa_post_" + kind,
    )(z, z, z, z, z, z, z, q_gain.reshape(1, B_HD).astype(F32), k_gain.reshape(1, B_HD).astype(F32), *states)


def _for_each_group(n, body):
    width = CHUNKS_PER_TRIP

    def group(jj, c):
        body(width * jj, width)
        return c

    lax.fori_loop(0, n // width, group, 0)
    done = (n // width) * width
    w = width // 2
    while w >= 1:
        take = (n - done) % (2 * w) >= w

        @pl.when(take)
        def _(done=done, w=w):
            body(done, w)
        done = done + jnp.where(take, w, 0)
        w //= 2


def _lane_fold(x, op):
    r = x[:, :LANES]
    for c in range(1, x.shape[1] // LANES):
        r = op(r, x[:, c * LANES:(c + 1) * LANES])
    return r


def _attend(streams, v_of, s_ref, mx_ref, ls_ref, acc_ref, qi, tq):
    mx_ref[...] = jnp.full(mx_ref.shape, NEG, F32)

    def score(j, count, delta=None):
        off = pl.multiple_of(j * tq, tq)
        n = count * tq
        r0 = 0
        for q, k_of, near_bias, far_bias in streams:
            rows = slice(r0, r0 + q.shape[0])
            r0 += q.shape[0]
            s = lax.dot_general(q, k_of(off, n), _NT, preferred_element_type=F32)
            b = far_bias(off, n) if delta is None else near_bias(delta, off)
            if b is not None:
                s = s + b
            s_ref[rows, pl.ds(off, n)] = s
            mx_ref[rows, :] = jnp.maximum(mx_ref[rows, :], _lane_fold(s, jnp.maximum))

    _for_each_group(jnp.maximum(qi - 1, 0), score)

    @pl.when(qi >= 1)
    def _():
        score(qi - 1, 1, 1)

    score(qi, 1, 0)

    m = jnp.max(mx_ref[...], axis=1, keepdims=True)
    mx_ref[...] = jnp.broadcast_to(m, mx_ref.shape)
    ls_ref[...] = jnp.zeros_like(ls_ref)
    acc_ref[...] = jnp.zeros_like(acc_ref)

    def pv(j, count):
        off = pl.multiple_of(j * tq, tq)
        n = count * tq
        s = s_ref[:, pl.ds(off, n)]
        m_rep = mx_ref[...]
        parts = [jnp.exp2(s[:, cc * LANES:(cc + 1) * LANES] - m_rep) for cc in range(n // LANES)]
        lsum = parts[0]
        for part in parts[1:]:
            lsum = lsum + part
        ls_ref[...] += lsum
        p = jnp.concatenate(parts, axis=1).astype(BF16)
        acc_ref[...] += jnp.dot(p, v_of(off, n), preferred_element_type=F32)

    _for_each_group(qi + 1, pv)
    return acc_ref[...] / jnp.sum(ls_ref[...], axis=1, keepdims=True)


def _diff_lambda(lp, lam_init):
    s1 = jnp.sum(lp[0:1] * lp[1:2], axis=1, keepdims=True)
    s2 = jnp.sum(lp[2:3] * lp[3:4], axis=1, keepdims=True)
    return jnp.exp(s1) - jnp.exp(s2) + lam_init


def _diff_attn_kernel(q_ref, k_ref, v_ref, bias_ref, lamp_ref, sub_ref, o_ref,
                      s_ref, mx_ref, ls_ref, acc_ref, *, lam_init, tq):
    qi = pl.program_id(2)
    streams = []
    for mp in range(2):
        sl = slice(mp * A_HD, (mp + 1) * A_HD)
        streams.append((q_ref[:, sl],
                        lambda off, n, sl=sl: k_ref[pl.ds(off, n), sl],
                        lambda delta, off, mp=mp: bias_ref[mp, delta * tq:(delta + 1) * tq, :],
                        lambda off, n: None))
    o2 = _attend(streams, lambda off, n: v_ref[pl.ds(off, n), :], s_ref, mx_ref, ls_ref, acc_ref, qi, tq)
    lam = _diff_lambda(lamp_ref[...], lam_init)
    o = o2[:tq] - lam * o2[tq:]
    o_ref[...] = (_head_rms(o, sub_ref[...]) * (1.0 - lam_init)).astype(o_ref.dtype)


def _diff_attn_prompt(q, kb, vb, bias, lam_p, subln, *, batch, t_pad, lam_init):
    tq = ATT_TILE
    nq = t_pad // tq
    hw = 2 * A_HD
    return pl.pallas_call(
        functools.partial(_diff_attn_kernel, lam_init=lam_init, tq=tq),
        grid=(batch, A_HEADS, nq),
        in_specs=[pl.BlockSpec((tq, hw), lambda b, h, i: (b * nq + i, h)),
                  pl.BlockSpec((t_pad, hw), lambda b, h, i: (b, h)),
                  pl.BlockSpec((t_pad, hw), lambda b, h, i: (b, h)),
                  pl.BlockSpec((2, 2 * tq, tq), lambda b, h, i: (h, 0, 0)),
                  pl.BlockSpec((4, A_HD), lambda b, h, i: (0, 0)),
                  pl.BlockSpec((1, hw), lambda b, h, i: (0, 0))],
        out_specs=pl.BlockSpec((tq, hw), lambda b, h, i: (b * nq + i, h)),
        out_shape=jax.ShapeDtypeStruct((batch * t_pad, A_QK), BF16),
        scratch_shapes=[pltpu.VMEM((2 * tq, t_pad), F32), pltpu.VMEM((2 * tq, LANES), F32),
                        pltpu.VMEM((2 * tq, LANES), F32), pltpu.VMEM((2 * tq, hw), F32)],
        compiler_params=_cparams(("parallel", "parallel", "arbitrary")),
        name="diff_attn_prompt",
    )(q, kb, vb, bias, lam_p.astype(F32), subln.reshape(1, hw).astype(F32))


def _sortable_key(x):
    bits = lax.bitcast_convert_type(x, I32)
    return bits ^ ((bits >> 31) & 0x7FFFFFFF)


def _kth_largest(count_ge, n_sel, shape, bits):
    zero = jnp.zeros(shape, I32)
    t = jnp.where(count_ge(zero) >= n_sel, zero, jnp.full(shape, -(1 << (bits - 1)), I32))

    def bit_body(i, t):
        cand = t + jnp.left_shift(jnp.int32(1), bits - 2 - i)
        return jnp.where(count_ge(cand) >= n_sel, cand, t)

    return lax.fori_loop(0, bits - 1, bit_body, t)


def _kth_largest_key(count_ge, n_sel, shape):
    return _kth_largest(count_ge, n_sel, shape, 32)


def _dsa_kernel(q_ref, iq_ref, iw_ref, k_ref, v_ref, ik_ref, bias_ref, o_ref,
                s_ref, mb_ref, hi_ref, lo_ref, mx_ref, ls_ref, acc_ref, *, tq, n_sel):
    qi = pl.program_id(1)
    n = pl.program_id(2)

    def keys_at(off):
        return lax.bitcast_convert_type(s_ref[0:tq, pl.ds(off, tq)], I32)

    @pl.when(n == 0)
    def _():
        row = lax.broadcasted_iota(I32, (tq, tq), 0)
        colm = lax.broadcasted_iota(I32, (tq, tq), 1)

        def score_body(j, c):
            off = pl.multiple_of(j * tq, tq)
            ikb = ik_ref[pl.ds(off, tq), :]
            acc = jnp.zeros((tq, tq), F32)
            for h in range(IDX_HEADS):
                d = lax.dot_general(iq_ref[:, h * IDX_HD:(h + 1) * IDX_HD], ikb, _NT,
                                    preferred_element_type=F32)
                acc = acc + jnp.maximum(d, 0.0) * iw_ref[:, h:h + 1]
            lim = jnp.where(j < qi, tq, 0)
            acc = jnp.where(colm <= row + lim, acc, -jnp.inf)
            key = _sortable_key(acc)
            s_ref[0:tq, pl.ds(off, tq)] = lax.bitcast_convert_type(key, F32)
            hi_ref[:, pl.ds(off, tq)] = (key >> 16).astype(I16)
            lo_ref[:, pl.ds(off, tq)] = ((key & 0xFFFF) - (1 << 15)).astype(I16)
            return c

        lax.fori_loop(0, qi + 1, score_body, 0)

        def count16(plane_ref, pred):
            def col_body(j, pc):
                blk = plane_ref[:, pl.ds(pl.multiple_of(j * tq, tq), tq)]
                ind = jnp.where(pred(blk), jnp.int16(1), jnp.int16(0))
                return pc + _lane_fold(ind, jnp.add)
            pc = lax.fori_loop(0, qi + 1, col_body, jnp.zeros((tq, LANES), I16))
            return jnp.sum(pc.astype(I32), axis=1, keepdims=True)

        t_hi = _kth_largest(lambda c: count16(hi_ref, lambda b: b >= c.astype(I16)), n_sel, (tq, 1), 16)
        t_hi16 = t_hi.astype(I16)
        need = n_sel - count16(hi_ref, lambda b: b > t_hi16)

        def tie_body(j, c):
            sl = pl.ds(pl.multiple_of(j * tq, tq), tq)
            lo_ref[:, sl] = jnp.where(hi_ref[:, sl] == t_hi16, lo_ref[:, sl], jnp.int16(-(1 << 15)))
            return c

        lax.fori_loop(0, qi + 1, tie_body, 0)
        t_lo = _kth_largest(lambda c: count16(lo_ref, lambda b: b >= c.astype(I16)), need, (tq, 1), 16)
        thr = (t_hi << 16) | (t_lo + (1 << 15))

        def tie_count_body(j, c):
            key = keys_at(pl.multiple_of(j * tq, tq))
            gt = _lane_fold(jnp.where(key > thr, 1, 0), jnp.add)
            eq = _lane_fold(jnp.where(key == thr, 1, 0), jnp.add)
            return c[0] + gt, c[1] + eq

        zeros = jnp.zeros((tq, LANES), I32)
        gt, eq = lax.fori_loop(0, qi + 1, tie_count_body, (zeros, zeros))
        room = n_sel - jnp.sum(gt, axis=1, keepdims=True)
        surplus = jnp.max(jnp.sum(eq, axis=1, keepdims=True) - room) > 0

        @pl.when(jnp.logical_not(surplus))
        def _():
            def mask_body(j, c):
                off = pl.multiple_of(j * tq, tq)
                mb_ref[:, pl.ds(off, tq)] = jnp.where(keys_at(off) >= thr, 0.0, NEG)
                return c

            lax.fori_loop(0, qi + 1, mask_body, 0)

        @pl.when(surplus)
        def _():
            def rank_body(j, c):
                off = pl.multiple_of(j * tq, tq)
                rank = (1 << 15) - 1 - (colm + off)
                hi_ref[:, pl.ds(off, tq)] = jnp.where(keys_at(off) == thr, rank, -(1 << 15)).astype(I16)
                return c

            lax.fori_loop(0, qi + 1, rank_body, 0)
            r_last = _kth_largest(lambda c: count16(hi_ref, lambda b: b >= c.astype(I16)), room, (tq, 1), 16)
            col_last = (1 << 15) - 1 - r_last

            def mask_body(j, c):
                off = pl.multiple_of(j * tq, tq)
                key = keys_at(off)
                keep = (key > thr) | ((key == thr) & (colm + off <= col_last))
                mb_ref[:, pl.ds(off, tq)] = jnp.where(keep, 0.0, NEG)
                return c

            lax.fori_loop(0, qi + 1, mask_body, 0)

    hp = DSA_HEADS_PER_PASS
    for g0 in range(0, B_GROUP, hp):
        gs = range(g0, g0 + hp)
        q = jnp.concatenate([q_ref[:, g * B_HD:(g + 1) * B_HD] for g in gs], axis=0)

        def far_bias(off, n):
            mb = mb_ref[:, pl.ds(off, n)]
            return jnp.concatenate([mb] * hp, axis=0)

        def near_bias(delta, off, gs=gs):
            mb = mb_ref[:, pl.ds(off, tq)]
            return jnp.concatenate([mb + bias_ref[g, delta * tq:(delta + 1) * tq, :] for g in gs], axis=0)

        o = _attend([(q, lambda off, n: k_ref[pl.ds(off, n), :], near_bias, far_bias)],
                    lambda off, n: v_ref[pl.ds(off, n), :], s_ref, mx_ref, ls_ref, acc_ref, qi, tq)
        for i, g in enumerate(gs):
            o_ref[:, g * B_HD:(g + 1) * B_HD] = o[i * tq:(i + 1) * tq].astype(o_ref.dtype)


def _dsa_prompt(q, iq, iw, kb, vb, ikb, bias, *, batch, t_pad, n_sel):
    tq = ATT_TILE
    hp = DSA_HEADS_PER_PASS
    nq = t_pad // tq
    gw = B_GROUP * B_HD
    return pl.pallas_call(
        functools.partial(_dsa_kernel, tq=tq, n_sel=n_sel),
        grid=(batch, nq, B_KV_HEADS),
        in_specs=[pl.BlockSpec((tq, gw), lambda b, i, n: (b * nq + i, n)),
                  pl.BlockSpec((tq, I_Q), lambda b, i, n: (b * nq + i, 0)),
                  pl.BlockSpec((tq, LANES), lambda b, i, n: (b * nq + i, 0)),
                  pl.BlockSpec((t_pad, B_HD), lambda b, i, n: (b, n)),
                  pl.BlockSpec((t_pad, B_HD), lambda b, i, n: (b, n)),
                  pl.BlockSpec((t_pad, IDX_HD), lambda b, i, n: (b, 0)),
                  pl.BlockSpec((B_GROUP, 2 * tq, tq), lambda b, i, n: (n, 0, 0))],
        out_specs=pl.BlockSpec((tq, gw), lambda b, i, n: (b * nq + i, n)),
        out_shape=jax.ShapeDtypeStruct((batch * t_pad, B_Q), BF16),
        scratch_shapes=[pltpu.VMEM((hp * tq, t_pad), F32), pltpu.VMEM((tq, t_pad), F32),
                        pltpu.VMEM((tq, t_pad), I16), pltpu.VMEM((tq, t_pad), I16),
                        pltpu.VMEM((hp * tq, LANES), F32), pltpu.VMEM((hp * tq, LANES), F32),
                        pltpu.VMEM((hp * tq, B_HD), F32)],
        compiler_params=_cparams(("parallel", "arbitrary", "arbitrary"), DSA_VMEM_LIMIT_BYTES),
        name="dsa_prompt",
    )(q, iq, iw, kb, vb, ikb, bias)


def _page_specs(comps, layer, n_pages, npp):
    def spec(i):
        return pl.BlockSpec((1, 1, PAGE_SIZE * comps, LANES),
                            lambda b, p, pt, i=i: (layer, pt[b * n_pages + p * npp + i], 0, 0))
    return [spec(i) for i in range(npp)]


def _component(ref, c, comps):
    return ref[0, 0, pl.ds(c, PAGE_SIZE, stride=comps), :].astype(BF16)


def _sample_attn_body(p, npg, npp, qbd, k_refs, v_refs, kcomps, vcomps, v_ids, knew_ref, vnew_ref,
                      blast_ref, bnew_ref, m_ref, l_ref, acc_ref, mask_of):
    groups = len(v_ids)
    rg = LANES // groups
    vw = acc_ref.shape[1]

    @pl.when(p == 0)
    def _():
        m_ref[...] = jnp.full(m_ref.shape, NEG, F32)
        l_ref[...] = jnp.zeros_like(l_ref)
        acc_ref[...] = jnp.zeros_like(acc_ref)

    def scores(xcat, bias, mask_idx):
        s = lax.dot_general(qbd, xcat, _NT, preferred_element_type=F32)
        if bias is not None:
            s = s + bias
        return mask_of(mask_idx, s)

    def update(s_blocks, v_groups):
        s = jnp.concatenate(s_blocks, axis=1) if len(s_blocks) > 1 else s_blocks[0]
        m_old = m_ref[...]
        m_new = jnp.maximum(m_old, jnp.max(s, axis=1, keepdims=True))
        alpha = jnp.exp2(m_old - m_new)
        pr = jnp.exp2(s - m_new)
        l_ref[...] = alpha * l_ref[...] + jnp.sum(pr, axis=1, keepdims=True)
        m_ref[...] = m_new
        pb = pr.astype(BF16)
        for g in range(groups):
            rows = slice(g * rg, (g + 1) * rg)
            pv = None
            for i, v_group in enumerate(v_groups):
                d = jnp.dot(pb[rows, i * PAGE_SIZE:(i + 1) * PAGE_SIZE], v_group(g),
                            preferred_element_type=F32)
                pv = d if pv is None else pv + d
            acc_ref[rows, :] = alpha[rows] * acc_ref[rows, :] + pv

    is_last = p == npg - 1
    s_blocks, v_groups = [], []
    for i in range(npp):
        xcat = jnp.concatenate([_component(k_refs[i], c, kcomps) for c in range(kcomps)], axis=1)

        def v_group(g, i=i):
            parts = [_component(v_refs[i], c, vcomps) for c in v_ids[g]]
            return parts[0] if len(parts) == 1 else jnp.concatenate(parts, axis=1)

        bias = jnp.where(is_last, 1.0, 0.0) * blast_ref[...] if i == npp - 1 else None
        s_blocks.append(scores(xcat, bias, i))
        v_groups.append(v_group)
    update(s_blocks, v_groups)

    @pl.when(is_last)
    def _():
        update([scores(knew_ref[0], bnew_ref[...], npp)],
               [lambda g: vnew_ref[0, :, g * vw:(g + 1) * vw]])


def _diff_sample_kernel(pt_ref, q_ref, *refs, npp, npg, lam_init, n_new):
    k_refs = refs[:npp]
    v_refs = refs[npp:2 * npp]
    (knew_ref, vnew_ref, blast_ref, bnew_ref, lamp_ref, sub_ref,
     o_ref, m_ref, l_ref, acc_ref) = refs[2 * npp:]
    p = pl.program_id(1)
    comps = 2 * A_HEADS
    v_ids = [(h, A_HEADS + h) for h in range(A_HEADS)]
    _sample_attn_body(p, npg, npp, q_ref[0], k_refs, v_refs, comps, comps, v_ids, knew_ref, vnew_ref,
                      blast_ref, bnew_ref, m_ref, l_ref, acc_ref, lambda i, s: s)

    @pl.when(p == npg - 1)
    def _():
        lam = _diff_lambda(lamp_ref[...], lam_init)
        o = acc_ref[...] / l_ref[...]
        hw = 2 * A_HD
        for h in range(A_HEADS):
            r = 2 * h * SUBLANES
            od = o[r:r + SUBLANES] - lam * o[r + SUBLANES:r + 2 * SUBLANES]
            od = _head_rms(od, sub_ref[...]) * (1.0 - lam_init)
            o_ref[0, :, h * hw:(h + 1) * hw] = od[:n_new]


def _diff_attn_sample(qbd, cache_k, cache_v, pt_flat, knew, vnew, blast, bnew, lam_p, subln,
                      *, layer, n_pages, lam_init, n_new):
    db = qbd.shape[0]
    npp = _pages_per_step(n_pages, DIFF_SAMPLE_PAGES)
    npg = n_pages // npp
    hw = 2 * A_HD
    comps = 2 * A_HEADS
    per_b = lambda shape: pl.BlockSpec((1,) + shape, lambda b, p, pt: (b, 0, 0))
    const = lambda shape: pl.BlockSpec(shape, lambda b, p, pt: (0, 0))
    grid_spec = pltpu.PrefetchScalarGridSpec(
        num_scalar_prefetch=1,
        grid=(db, npg),
        in_specs=([per_b((LANES, A_QK))] + _page_specs(comps, layer, n_pages, npp)
                  + _page_specs(comps, layer, n_pages, npp)
                  + [per_b((PAGE_SIZE, A_QK)), per_b((PAGE_SIZE, A_QK)),
                     const((LANES, PAGE_SIZE)), const((LANES, PAGE_SIZE)),
                     const((4, A_HD)), const((1, hw))]),
        out_specs=pl.BlockSpec((1, n_new, A_QK), lambda b, p, pt: (b, 0, 0)),
        scratch_shapes=[pltpu.VMEM((LANES, 1), F32), pltpu.VMEM((LANES, 1), F32),
                        pltpu.VMEM((LANES, hw), F32)],
    )
    return pl.pallas_call(
        functools.partial(_diff_sample_kernel, npp=npp, npg=npg, lam_init=lam_init, n_new=n_new),
        grid_spec=grid_spec,
        out_shape=jax.ShapeDtypeStruct((db, n_new, A_QK), F32),
        compiler_params=_cparams(("parallel", "arbitrary")),
        name="diff_attn_sample",
    )(pt_flat, qbd, *([cache_k] * npp), *([cache_v] * npp), knew, vnew, blast, bnew,
      lam_p.astype(F32), subln.reshape(1, hw).astype(F32))


def _idx_sample_kernel(pt_ref, iq_ref, w_ref, *refs, npp, npg, n_pages, n_sel, n_new):
    ik_refs = refs[:npp]
    iknew_ref, keys_ref, thr_ref = refs[npp:]
    p = pl.program_id(1)
    iq = iq_ref[0]
    w = w_ref[0]

    def slot_scores(ik):
        d = lax.dot_general(iq, ik, _NT, preferred_element_type=F32)
        r = jnp.maximum(d, 0.0) * w
        return jnp.sum(r.reshape(SUBLANES, IDX_HEADS, PAGE_SIZE), axis=1)

    for i in range(npp):
        sc = slot_scores(ik_refs[i][0, 0].astype(BF16))
        off = pl.multiple_of((p * npp + i) * PAGE_SIZE, PAGE_SIZE)
        keys_ref[0, :, pl.ds(off, PAGE_SIZE)] = _sortable_key(sc)

    @pl.when(p == npg - 1)
    def _():
        sc = slot_scores(iknew_ref[0])
        srow = lax.broadcasted_iota(I32, sc.shape, 0)
        tcol = lax.broadcasted_iota(I32, sc.shape, 1)
        sc = jnp.where((tcol <= srow) & (tcol < n_new), sc, -jnp.inf)
        keys_ref[0, :, n_pages * PAGE_SIZE:(n_pages + 1) * PAGE_SIZE] = _sortable_key(sc)

        def count_ge(cand):
            return jnp.sum(jnp.where(keys_ref[0] >= cand, 1, 0), axis=1, keepdims=True)

        t = _kth_largest_key(count_ge, n_sel, (SUBLANES, 1))
        thr_ref[0] = jnp.broadcast_to(t, (SUBLANES, LANES))

        keys = keys_ref[0]
        col = lax.broadcasted_iota(I32, keys.shape, 1)
        room = n_sel - jnp.sum(jnp.where(keys > t, 1, 0), axis=1, keepdims=True)
        tie = keys == t
        rank = jnp.where(tie, (1 << 15) - 1 - col, -(1 << 15))
        r_last = _kth_largest(lambda c: jnp.sum(jnp.where(rank >= c, 1, 0), axis=1, keepdims=True),
                              room, (SUBLANES, 1), 16)
        col_last = (1 << 15) - 1 - r_last
        keys_ref[0] = jnp.where(tie & (col > col_last), keys - 1, keys)


def _idx_sample(iqs, ws, cache_ki, pt_flat, iknew, *, layer, n_pages, n_sel, n_new):
    db = iqs.shape[0]
    npp = _pages_per_step(n_pages, IDX_SAMPLE_PAGES)
    npg = n_pages // npp
    width = (n_pages + 1) * PAGE_SIZE
    per_b = lambda shape: pl.BlockSpec((1,) + shape, lambda b, p, pt: (b, 0, 0))
    grid_spec = pltpu.PrefetchScalarGridSpec(
        num_scalar_prefetch=1,
        grid=(db, npg),
        in_specs=([per_b((SUBLANES * IDX_HEADS, IDX_HD)), per_b((SUBLANES * IDX_HEADS, 1))]
                  + _page_specs(1, layer, n_pages, npp) + [per_b((PAGE_SIZE, IDX_HD))]),
        out_specs=[per_b((SUBLANES, width)), per_b((SUBLANES, LANES))],
    )
    return pl.pallas_call(
        functools.partial(_idx_sample_kernel, npp=npp, npg=npg, n_pages=n_pages, n_sel=n_sel, n_new=n_new),
        grid_spec=grid_spec,
        out_shape=[jax.ShapeDtypeStruct((db, SUBLANES, width), I32),
                   jax.ShapeDtypeStruct((db, SUBLANES, LANES), I32)],
        compiler_params=_cparams(("parallel", "arbitrary")),
        name="dsa_index_sample",
    )(pt_flat, iqs, ws, *([cache_ki] * npp), iknew)


def _dsa_sample_kernel(pt_ref, q_ref, *refs, npp, npg, n_new):
    k_refs = refs[:npp]
    v_refs = refs[npp:2 * npp]
    key_refs = refs[2 * npp:3 * npp + 1]
    (thr_ref, knew_ref, vnew_ref, blast_ref, bnew_ref,
     o_ref, m_ref, l_ref, acc_ref) = refs[3 * npp + 1:]
    p = pl.program_id(1)

    def mask_of(i, s):
        keys = jnp.concatenate([key_refs[i][0]] * B_HEADS, axis=0)
        thr = jnp.concatenate([thr_ref[0]] * B_HEADS, axis=0)
        return jnp.where(keys >= thr, s, NEG)

    v_ids = [(n,) for n in range(B_KV_HEADS)]
    _sample_attn_body(p, npg, npp, q_ref[0], k_refs, v_refs, B_KV_HEADS, B_KV_HEADS, v_ids, knew_ref,
                      vnew_ref, blast_ref, bnew_ref, m_ref, l_ref, acc_ref, mask_of)

    @pl.when(p == npg - 1)
    def _():
        o = acc_ref[...] / l_ref[...]
        for hd in range(B_HEADS):
            o_ref[0, :, hd * B_HD:(hd + 1) * B_HD] = o[hd * SUBLANES:hd * SUBLANES + n_new]


def _dsa_attn_sample(qbd, cache_k, cache_v, keys, thr, pt_flat, knew, vnew, blast, bnew,
                     *, layer, n_pages, n_new):
    db = qbd.shape[0]
    npp = _pages_per_step(n_pages, DSA_SAMPLE_PAGES)
    npg = n_pages // npp
    per_b = lambda shape: pl.BlockSpec((1,) + shape, lambda b, p, pt: (b, 0, 0))
    const = lambda shape: pl.BlockSpec(shape, lambda b, p, pt: (0, 0))
    key_specs = [pl.BlockSpec((1, SUBLANES, PAGE_SIZE), lambda b, p, pt, i=i: (b, 0, p * npp + i))
                 for i in range(npp)]
    key_specs.append(pl.BlockSpec((1, SUBLANES, PAGE_SIZE), lambda b, p, pt: (b, 0, n_pages)))
    grid_spec = pltpu.PrefetchScalarGridSpec(
        num_scalar_prefetch=1,
        grid=(db, npg),
        in_specs=([per_b((LANES, B_KV))] + _page_specs(B_KV_HEADS, layer, n_pages, npp)
                  + _page_specs(B_KV_HEADS, layer, n_pages, npp) + key_specs
                  + [per_b((SUBLANES, LANES)), per_b((PAGE_SIZE, B_KV)), per_b((PAGE_SIZE, B_KV)),
                     const((LANES, PAGE_SIZE)), const((LANES, PAGE_SIZE))]),
        out_specs=pl.BlockSpec((1, n_new, B_Q), lambda b, p, pt: (b, 0, 0)),
        scratch_shapes=[pltpu.VMEM((LANES, 1), F32), pltpu.VMEM((LANES, 1), F32),
                        pltpu.VMEM((LANES, B_HD), F32)],
    )
    return pl.pallas_call(
        functools.partial(_dsa_sample_kernel, npp=npp, npg=npg, n_new=n_new),
        grid_spec=grid_spec,
        out_shape=jax.ShapeDtypeStruct((db, n_new, B_Q), F32),
        compiler_params=_cparams(("parallel", "arbitrary")),
        name="dsa_attn_sample",
    )(pt_flat, qbd, *([cache_k] * npp), *([cache_v] * npp), *([keys] * (npp + 1)), thr, knew, vnew,
      blast, bnew)


def _pad_rows(x, rows):
    return jnp.pad(x, ((0, 0), (0, rows - x.shape[1]), (0, 0)))


def kernel(x_prompt, x_sample, cache_a_k, cache_a_v, cache_b_k, cache_b_v, cache_b_kidx, page_table, meta_tokens, rel_bias_table, norm_mix, norm_ffn, a_w_in, a_w_out, a_q_norm, a_k_norm, a_lambda, a_subln, b_w_in, b_w_out, b_q_norm, b_k_norm, ffn_w1, ffn_w2):
    batch, seq, d_model = x_prompt.shape
    db, ds, _ = x_sample.shape
    depth = norm_mix.shape[0]
    n_pages = page_table.shape[1]
    past_len = n_pages * PAGE_SIZE
    t_real = seq + N_META
    t_pad = -(-t_real // Q_BLOCK) * Q_BLOCK
    tq = ATT_TILE
    n_s = db * ds
    r0 = batch * t_pad
    m_tot = r0 + TAIL_ROWS
    assert t_pad % tq == 0 and r0 % TAIL_ROWS == 0 and n_s <= TAIL_ROWS and ds <= SUBLANES
    n_sel_prompt = min(TOPK_MAX, seq // 4)
    n_sel_sample = min(TOPK_MAX, (past_len + ds) // 4)

    meta = jnp.broadcast_to(meta_tokens.astype(F32), (batch, N_META, d_model))
    xp = jnp.concatenate([meta, x_prompt, jnp.zeros((batch, t_pad - t_real, d_model), F32)], axis=1)
    x = jnp.concatenate([xp.reshape(r0, d_model), x_sample.reshape(n_s, d_model),
                         jnp.zeros((TAIL_ROWS - n_s, d_model), F32)], axis=0)

    ri = lax.broadcasted_iota(I32, (tq, tq), 0)
    ci = lax.broadcasted_iota(I32, (tq, tq), 1)
    r8 = lax.broadcasted_iota(I32, (SUBLANES, tq), 0)
    c8 = lax.broadcasted_iota(I32, (SUBLANES, tq), 1)
    dist = jnp.concatenate([ri - ci, tq + ri - ci, PAGE_SIZE + r8 - c8, r8 - c8], axis=0)
    bias_all = _bias_tiles(rel_bias_table, dist)
    bias_prompt = bias_all[:, :2 * tq]
    b_last = bias_all[:, 2 * tq:2 * tq + SUBLANES, :PAGE_SIZE].reshape(N_BIAS_HEADS * SUBLANES, PAGE_SIZE)
    b_new = bias_all[:, 2 * tq + SUBLANES:2 * tq + 2 * SUBLANES, :PAGE_SIZE].reshape(
        N_BIAS_HEADS * SUBLANES, PAGE_SIZE)

    pt_flat = page_table.reshape(-1).astype(I32)
    n_a, n_pool = cache_a_k.shape[:2]
    n_b = cache_b_k.shape[0]
    cache_a_k2 = cache_a_k.reshape(n_a, n_pool, PAGE_SIZE * 2 * A_HEADS, A_HD)
    cache_a_v2 = cache_a_v.reshape(n_a, n_pool, PAGE_SIZE, A_HEADS, 2, A_HD).transpose(
        0, 1, 2, 4, 3, 5).reshape(n_a, n_pool, PAGE_SIZE * 2 * A_HEADS, A_HD)
    cache_b_k2 = cache_b_k.reshape(n_b, n_pool, PAGE_SIZE * B_KV_HEADS, B_HD)
    cache_b_v2 = cache_b_v.reshape(n_b, n_pool, PAGE_SIZE * B_KV_HEADS, B_HD)

    b_in_pad = -(-B_IN // 256) * 256
    wb_in = jnp.pad(b_w_in, ((0, 0), (0, 0), (0, b_in_pad - B_IN))).astype(BF16)
    w_ffn2 = ffn_w2.astype(BF16)

    def mm(a, w, layer, **kw):
        if a.shape[1] <= RESIDENT_MAX_K:
            return _matmul_resident(a, w, layer, **kw)
        return _matmul(a, w.astype(BF16), layer, **kw)
    eye_m = jnp.eye(2 * A_HEADS, dtype=BF16)
    eye_n = jnp.eye(B_KV_HEADS, dtype=BF16)

    def rows_of(a):
        return a[:n_s].reshape((db, ds) + a.shape[1:])

    n_a_layers = (depth + N_MIXERS - 1) // N_MIXERS
    n_b_layers = depth // N_MIXERS
    a_states, b_states = (), ()
    outs = {k: [] for k in ("aks", "avs", "bks", "bvs", "bis")}
    for i in range(depth):
        j = i // N_MIXERS
        h = _rmsnorm(x, norm_mix[i])
        if i % N_MIXERS == 0:
            lam_init = 0.8 - 0.6 * math.exp(-0.3 * i)
            z = mm(h, a_w_in, j)
            dims = (batch, t_pad, t_real, r0, n_a_layers, j)
            q, k_st, kb, vb, v_st = _a_post(z, a_q_norm[j], a_k_norm[j], "prompt", dims, a_states)
            a_states = (k_st, v_st)
            if t_real % POST_TILE:
                _, k_st, _, _, v_st = _a_post(z, a_q_norm[j], a_k_norm[j], "edge", dims, a_states)
                a_states = (k_st, v_st)
            q_t, k_t, kb_t, vb_t, v_t = _a_post(z, a_q_norm[j], a_k_norm[j], "tail", dims)
            o_p = _diff_attn_prompt(q, kb, vb, bias_prompt, a_lambda[j], a_subln[j],
                                    batch=batch, t_pad=t_pad, lam_init=lam_init)
            qs = _pad_rows(rows_of(q_t), SUBLANES).reshape(db, SUBLANES, 2 * A_HEADS, A_HD)
            qbd = (qs[:, None] * eye_m[None, :, None, :, None]).reshape(db, LANES, A_QK)
            o_s = _diff_attn_sample(qbd, cache_a_k2, cache_a_v2, pt_flat,
                                    _pad_rows(rows_of(kb_t), PAGE_SIZE), _pad_rows(rows_of(vb_t), PAGE_SIZE),
                                    b_last, b_new, a_lambda[j], a_subln[j],
                                    layer=j, n_pages=n_pages, lam_init=lam_init, n_new=ds)
            w_out = a_w_out
            outs["aks"].append(rows_of(k_t.reshape(TAIL_ROWS, A_HEADS, 2, A_HD)))
            outs["avs"].append(rows_of(v_t.reshape(TAIL_ROWS, 2, A_HEADS, A_HD).transpose(0, 2, 1, 3)
                                       .reshape(TAIL_ROWS, A_HEADS, 2 * A_HD)))
        else:
            z = _matmul(h, wb_in, j)
            dims = (batch, t_pad, t_real, r0, n_b_layers, j)
            q, k_st, kb, vb, iq, ikb, iw, v_st, ik_st = _b_post(z, b_q_norm[j], b_k_norm[j], "prompt", dims,
                                                                b_states)
            b_states = (k_st, v_st, ik_st)
            if t_real % POST_TILE:
                edge = _b_post(z, b_q_norm[j], b_k_norm[j], "edge", dims, b_states)
                b_states = (edge[1], edge[7], edge[8])
            q_t, k_t, kb_t, vb_t, iq_t, ikb_t, iw_t, v_t, ik_t = _b_post(z, b_q_norm[j], b_k_norm[j], "tail", dims)
            o_p = _dsa_prompt(q, iq, iw, kb, vb, ikb, bias_prompt,
                              batch=batch, t_pad=t_pad, n_sel=n_sel_prompt)
            iqs = _pad_rows(rows_of(iq_t), SUBLANES).reshape(db, SUBLANES * IDX_HEADS, IDX_HD)
            ws = _pad_rows(rows_of(iw_t)[:, :, :IDX_HEADS], SUBLANES).reshape(db, SUBLANES * IDX_HEADS, 1)
            keys, thr = _idx_sample(iqs, ws, cache_b_kidx, pt_flat, _pad_rows(rows_of(ikb_t), PAGE_SIZE),
                                    layer=j, n_pages=n_pages, n_sel=n_sel_sample, n_new=ds)
            qs = _pad_rows(rows_of(q_t), SUBLANES).reshape(db, SUBLANES, B_KV_HEADS, B_GROUP, B_HD)
            qbd = (qs.transpose(0, 3, 1, 2, 4)[:, None]
                   * eye_n[None, :, None, None, :, None]).reshape(db, LANES, B_KV)
            o_s = _dsa_attn_sample(qbd, cache_b_k2, cache_b_v2, keys, thr, pt_flat,
                                   _pad_rows(rows_of(kb_t), PAGE_SIZE), _pad_rows(rows_of(vb_t), PAGE_SIZE),
                                   b_last, b_new, layer=j, n_pages=n_pages, n_new=ds)
            w_out = b_w_out
            outs["bks"].append(rows_of(k_t.reshape(TAIL_ROWS, B_KV_HEADS, B_HD)))
            outs["bvs"].append(rows_of(v_t.reshape(TAIL_ROWS, B_KV_HEADS, B_HD)))
            outs["bis"].append(rows_of(ik_t))
        width = o_p.shape[1]
        o = jnp.concatenate([o_p, o_s.reshape(n_s, width).astype(BF16),
                             jnp.zeros((TAIL_ROWS - n_s, width), BF16)], axis=0)
        x = mm(o, w_out, j, mode="res", res=x)
        hf = _rmsnorm(x, norm_ffn[i])
        u = mm(hf, ffn_w1, i, mode="relu2", out_dtype=BF16)
        x = _matmul(u, w_ffn2, i, mode="res", res=x)

    y_prompt = x[:r0].reshape(batch, t_pad, d_model)[:, N_META:t_real]
    y_sample = x[r0:r0 + n_s].reshape(db, ds, d_model)
    st = lambda key: jnp.stack(outs[key])
    a_k_p = a_states[0].reshape(n_a_layers, batch, t_real, A_HEADS, 2, A_HD)
    a_v_p = a_states[1].reshape(n_a_layers, batch, t_real, 2, A_HEADS, A_HD).transpose(
        0, 1, 2, 4, 3, 5).reshape(n_a_layers, batch, t_real, A_HEADS, 2 * A_HD)
    b_k_p = b_states[0].reshape(n_b_layers, batch, t_real, B_KV_HEADS, B_HD)
    b_v_p = b_states[1].reshape(n_b_layers, batch, t_real, B_KV_HEADS, B_HD)
    return (y_prompt, y_sample, a_k_p, a_v_p, b_k_p, b_v_p, b_states[2],
            st("aks"), st("avs"), st("bks"), st("bvs"), st("bis"))
```

```python
import functools
import math

import jax
import jax.numpy as jnp
from jax import lax
from jax.experimental import pallas as pl
from jax.experimental.pallas import tpu as pltpu

F32 = jnp.float32
BF16 = jnp.bfloat16
I32 = jnp.int32

N_META = 16
N_MIXERS = 2
A_HEADS = 8
A_HD = 128
A_QK = A_HEADS * 2 * A_HD
B_HEADS = 16
B_KV_HEADS = 4
B_GROUP = B_HEADS // B_KV_HEADS
B_HD = 128
B_Q = B_HEADS * B_HD
B_KV = B_KV_HEADS * B_HD
IDX_HEADS = 16
IDX_HD = 128
I_Q = IDX_HEADS * IDX_HD
B_IN = B_Q + 2 * B_KV + I_Q + IDX_HD + IDX_HEADS
TOPK_MAX = 256
N_BUCKETS = 32
MAX_DISTANCE = 128
N_BIAS_HEADS = 16
PAGE_SIZE = 128
Q_BLOCK = 128
EPS = 1e-6

LANES = 128
SUBLANES = 8
ATT_TILE = 384
RESIDENT_MAX_K = 2048
POST_TILE = 128
TAIL_ROWS = 256
NEG = -1e30
INT_MIN = -2 ** 31
LOG2E = math.log2(math.e)
DIFF_SAMPLE_PAGES = 4
DSA_SAMPLE_PAGES = 8
IDX_SAMPLE_PAGES = 16
CHUNKS_PER_TRIP = 4
DSA_HEADS_PER_PASS = 2
VMEM_LIMIT_BYTES = 48 * 1024 * 1024
DSA_VMEM_LIMIT_BYTES = 58 * 1024 * 1024
I16 = jnp.int16

_NT = (((1,), (1,)), ((), ()))


def _cparams(sem, vmem_limit=VMEM_LIMIT_BYTES):
    return pltpu.CompilerParams(dimension_semantics=sem, vmem_limit_bytes=vmem_limit)


def _pages_per_step(n_pages, want):
    return max(c for c in range(1, want + 1) if n_pages % c == 0)


def _pick(n, candidates):
    for c in candidates:
        if n % c == 0:
            return c
    raise ValueError(f"no tile for {n} in {candidates}")


def _rms_kernel(x_ref, g_ref, o_ref):
    x = x_ref[...]
    ms = jnp.mean(x * x, axis=-1, keepdims=True)
    o_ref[...] = (x * lax.rsqrt(ms + EPS) * g_ref[...]).astype(o_ref.dtype)


def _rmsnorm(x, g):
    m, d = x.shape
    tm = _pick(m, (512, 256))
    return pl.pallas_call(
        _rms_kernel,
        grid=(m // tm,),
        in_specs=[pl.BlockSpec((tm, d), lambda i: (i, 0)),
                  pl.BlockSpec((1, d), lambda i: (0, 0))],
        out_specs=pl.BlockSpec((tm, d), lambda i: (i, 0)),
        out_shape=jax.ShapeDtypeStruct((m, d), BF16),
        compiler_params=_cparams(("parallel",)),
        name="rmsnorm",
    )(x, g.reshape(1, d).astype(F32))


def _mm_kernel(*refs, nk, mode):
    if mode == "res":
        a_ref, w_ref, r_ref, o_ref, acc_ref = refs
    else:
        a_ref, w_ref, o_ref, acc_ref = refs
        r_ref = None
    k = pl.program_id(2)

    @pl.when(k == 0)
    def _():
        acc_ref[...] = jnp.zeros_like(acc_ref)

    acc_ref[...] += jnp.dot(a_ref[...], w_ref[...], preferred_element_type=F32)

    @pl.when(k == nk - 1)
    def _():
        acc = acc_ref[...]
        if mode == "relu2":
            r = jnp.maximum(acc, 0.0)
            acc = r * r
        elif mode == "res":
            acc = acc + r_ref[...]
        o_ref[...] = acc.astype(o_ref.dtype)


def _mm_resident_kernel(a_ref, w_ref, *rest, mode):
    if mode == "res":
        r_ref, o_ref, wb_ref = rest
    else:
        o_ref, wb_ref = rest

    @pl.when(pl.program_id(1) == 0)
    def _():
        wb_ref[...] = w_ref[...].astype(BF16)

    acc = jnp.dot(a_ref[...], wb_ref[...], preferred_element_type=F32)
    if mode == "relu2":
        r = jnp.maximum(acc, 0.0)
        acc = r * r
    elif mode == "res":
        acc = acc + r_ref[...]
    o_ref[...] = acc.astype(o_ref.dtype)


def _matmul_resident(a, w, layer, *, mode="plain", res=None, out_dtype=F32):
    m, kd = a.shape
    n = w.shape[2]
    tm = _pick(m, (1088, 512, 256))
    tn = _pick(n, (512, 256, 128) if mode == "res" else (1024, 512, 256, 128))
    in_specs = [pl.BlockSpec((tm, kd), lambda j, i: (i, 0)),
                pl.BlockSpec((None, kd, tn), lambda j, i: (layer, 0, j))]
    args = [a, w]
    if mode == "res":
        in_specs.append(pl.BlockSpec((tm, tn), lambda j, i: (i, j)))
        args.append(res)
    return pl.pallas_call(
        functools.partial(_mm_resident_kernel, mode=mode),
        grid=(n // tn, m // tm),
        in_specs=in_specs,
        out_specs=pl.BlockSpec((tm, tn), lambda j, i: (i, j)),
        out_shape=jax.ShapeDtypeStruct((m, n), out_dtype),
        scratch_shapes=[pltpu.VMEM((kd, tn), BF16)],
        compiler_params=_cparams(("parallel", "arbitrary")),
        name="matmul_resident_" + mode,
    )(*args)


def _matmul(a, w, layer, *, mode="plain", res=None, out_dtype=F32):
    m, kd = a.shape
    n = w.shape[2]
    tm = _pick(m, (1088, 512, 256))
    tn = _pick(n, (1024, 768, 512, 256, 128))
    tk = _pick(kd, (2048, 1024, 512, 256))
    nk = kd // tk
    in_specs = [pl.BlockSpec((tm, tk), lambda i, j, k: (i, k)),
                pl.BlockSpec((None, tk, tn), lambda i, j, k: (layer, k, j))]
    args = [a, w]
    if mode == "res":
        in_specs.append(pl.BlockSpec((tm, tn), lambda i, j, k: (i, j)))
        args.append(res)
    return pl.pallas_call(
        functools.partial(_mm_kernel, nk=nk, mode=mode),
        grid=(m // tm, n // tn, nk),
        in_specs=in_specs,
        out_specs=pl.BlockSpec((tm, tn), lambda i, j, k: (i, j)),
        out_shape=jax.ShapeDtypeStruct((m, n), out_dtype),
        scratch_shapes=[pltpu.VMEM((tm, tn), F32)],
        compiler_params=_cparams(("parallel", "parallel", "arbitrary")),
        name="matmul_" + mode,
    )(*args)


def _bias_kernel(tab_ref, d_ref, o_ref):
    h = pl.program_id(0)
    d = d_ref[...]
    max_exact = N_BUCKETS // 2
    n = jnp.maximum(d, 0)
    nf = jnp.maximum(n, 1).astype(F32)
    large = max_exact + (jnp.log(nf / max_exact) / math.log(MAX_DISTANCE / max_exact)
                         * (N_BUCKETS - max_exact)).astype(I32)
    bucket = jnp.where(n < max_exact, n, jnp.minimum(large, N_BUCKETS - 1))
    acc = jnp.zeros(d.shape, F32)
    for b in range(N_BUCKETS):
        acc = jnp.where(bucket == b, tab_ref[b * N_BIAS_HEADS + h], acc)
    far = tab_ref[(N_BUCKETS - 1) * N_BIAS_HEADS + h]
    o_ref[0] = jnp.where(d < 0, NEG, (acc - far) * LOG2E)


def _bias_tiles(table, dist):
    r, c = dist.shape
    return pl.pallas_call(
        _bias_kernel,
        grid=(N_BIAS_HEADS,),
        in_specs=[pl.BlockSpec(memory_space=pltpu.SMEM),
                  pl.BlockSpec((r, c), lambda h: (0, 0))],
        out_specs=pl.BlockSpec((1, r, c), lambda h: (h, 0, 0)),
        out_shape=jax.ShapeDtypeStruct((N_BIAS_HEADS, r, c), F32),
        compiler_params=_cparams(("arbitrary",)),
        name="bias_tiles",
    )(table.astype(F32).reshape(-1), dist)


def _head_rms(x, g):
    ms = jnp.mean(x * x, axis=-1, keepdims=True)
    return x * lax.rsqrt(ms + EPS) * g


def _post_specs(kind, batch, t_pad, t_real, r0, layers, layer):
    tm = POST_TILE
    tiles = t_pad // tm
    whole = t_real // tm
    rem = t_real - whole * tm
    state_rows, state_tiles = tm, None
    state_sds = lambda comps: jax.ShapeDtypeStruct((layers, batch, t_real * comps, LANES), F32)
    if kind == "prompt":
        grid = (batch, tiles)
        rows = lambda b, t: b * tiles + t
        n_rows = batch * t_pad
        state_tiles = whole
        state_spec = lambda comps: pl.BlockSpec((None, None, tm * comps, LANES),
                                                lambda b, t: (layer, b, jnp.minimum(t, whole - 1), 0))
    elif kind == "edge":
        assert rem % 16 == 0 and (whole * tm) % rem == 0
        grid = (batch, 1)
        rows = lambda b, t: b * tiles + whole
        n_rows = batch * tm
        state_rows = rem
        state_spec = lambda comps: pl.BlockSpec((None, None, rem * comps, LANES),
                                                lambda b, t: (layer, b, whole * tm // rem, 0))
    else:
        grid = (1, TAIL_ROWS // tm)
        rows = lambda b, t: r0 // tm + t
        n_rows = TAIL_ROWS
        state_spec = lambda comps: pl.BlockSpec((tm * comps, LANES), lambda b, t: (t, 0))
        state_sds = lambda comps: jax.ShapeDtypeStruct((TAIL_ROWS * comps, LANES), F32)
    in_blk = lambda w, c: pl.BlockSpec((tm, w), lambda b, t, c=c: (rows(b, t), c))
    out_blk = lambda w: pl.BlockSpec((tm, w), lambda b, t: (b * grid[1] + t, 0))
    out_sds = lambda w, dt: jax.ShapeDtypeStruct((n_rows, w), dt)
    return grid, in_blk, out_blk, out_sds, state_spec, state_sds, state_rows, state_tiles


def _when_state_tile(state_tiles, fn):
    if state_tiles is None:
        fn()
    else:
        pl.when(pl.program_id(1) < state_tiles)(fn)


def _a_post_kernel(zq_ref, zk_ref, zv_ref, qg_ref, kg_ref, *rest, state_rows, state_tiles):
    q_ref, k_ref, kb_ref, vb_ref, v_ref = rest[-5:]
    scale = A_HD ** -0.5 * LOG2E
    comps = 2 * A_HEADS
    for g in range(comps):
        sl = slice(g * A_HD, (g + 1) * A_HD)
        q_ref[:, sl] = (_head_rms(zq_ref[:, sl], qg_ref[...]) * scale).astype(BF16)
        kn = _head_rms(zk_ref[:, sl], kg_ref[...])
        kb_ref[:, sl] = kn.astype(BF16)
        head, half = g // 2, g % 2

        def store_state(g=g, sl=sl, kn=kn, c_v=half * A_HEADS + head):
            k_ref[pl.ds(g, state_rows, stride=comps), :] = kn[:state_rows]
            v_ref[pl.ds(c_v, state_rows, stride=comps), :] = zv_ref[:state_rows, sl]

        _when_state_tile(state_tiles, store_state)
    vb_ref[...] = zv_ref[...].astype(BF16)


def _a_post(z, q_gain, k_gain, kind, dims, states=()):
    comps = 2 * A_HEADS
    grid, in_blk, out_blk, out_sds, state_spec, state_sds, state_rows, state_tiles = _post_specs(kind, *dims)
    gain = pl.BlockSpec((1, A_HD), lambda b, t: (0, 0))
    return pl.pallas_call(
        functools.partial(_a_post_kernel, state_rows=state_rows, state_tiles=state_tiles),
        grid=grid,
        in_specs=([in_blk(A_QK, 0), in_blk(A_QK, 1), in_blk(A_QK, 2), gain, gain]
                  + [pl.BlockSpec(memory_space=pl.ANY)] * len(states)),
        out_specs=[out_blk(A_QK), state_spec(comps), out_blk(A_QK), out_blk(A_QK), state_spec(comps)],
        out_shape=[out_sds(A_QK, BF16), state_sds(comps), out_sds(A_QK, BF16), out_sds(A_QK, BF16),
                   state_sds(comps)],
        input_output_aliases=({5: 1, 6: 4} if states else {}),
        compiler_params=_cparams(("parallel", "parallel")),
        name="diff_post_" + kind,
    )(z, z, z, q_gain.reshape(1, A_HD).astype(F32), k_gain.reshape(1, A_HD).astype(F32), *states)


def _b_post_kernel(zq_ref, zk_ref, zv_ref, zi0_ref, zi1_ref, zik_ref, ziw_ref, qg_ref, kg_ref, *rest,
                   state_rows, state_tiles):
    q_ref, k_ref, kb_ref, vb_ref, iq_ref, ikb_ref, iw_ref, v_ref, ik_ref = rest[-9:]
    for g in range(B_HEADS):
        sl = slice(g * B_HD, (g + 1) * B_HD)
        q_ref[:, sl] = (_head_rms(zq_ref[:, sl], qg_ref[...]) * (B_HD ** -0.5 * LOG2E)).astype(BF16)
    for g in range(B_KV_HEADS):
        sl = slice(g * B_HD, (g + 1) * B_HD)
        kn = _head_rms(zk_ref[:, sl], kg_ref[...])
        kb_ref[:, sl] = kn.astype(BF16)

        def store_state(g=g, sl=sl, kn=kn):
            k_ref[pl.ds(g, state_rows, stride=B_KV_HEADS), :] = kn[:state_rows]
            v_ref[pl.ds(g, state_rows, stride=B_KV_HEADS), :] = zv_ref[:state_rows, sl]

        _when_state_tile(state_tiles, store_state)
    vb_ref[...] = zv_ref[...].astype(BF16)
    half = I_Q // 2
    iq_ref[:, :half] = (zi0_ref[...] * (IDX_HD ** -0.5)).astype(BF16)
    iq_ref[:, half:] = (zi1_ref[...] * (IDX_HD ** -0.5)).astype(BF16)

    def store_index_key():
        ik_ref[...] = zik_ref[:state_rows, :]

    _when_state_tile(state_tiles, store_index_key)
    ikb_ref[...] = zik_ref[...].astype(BF16)
    iw_ref[...] = ziw_ref[...] * (IDX_HEADS ** -0.5)


def _b_post(z, q_gain, k_gain, kind, dims, states=()):
    grid, blk, out, sds, state_spec, state_sds, state_rows, state_tiles = _post_specs(kind, *dims)
    gain = pl.BlockSpec((1, B_HD), lambda b, t: (0, 0))
    o1 = B_Q
    o2 = o1 + B_KV
    o3 = o2 + B_KV
    o4 = o3 + I_Q
    half = I_Q // 2
    kv = B_KV_HEADS
    return pl.pallas_call(
        functools.partial(_b_post_kernel, state_rows=state_rows, state_tiles=state_tiles),
        grid=grid,
        in_specs=([blk(B_Q, 0), blk(B_KV, o1 // B_KV), blk(B_KV, o2 // B_KV),
                   blk(half, o3 // half), blk(half, o3 // half + 1),
                   blk(IDX_HD, o4 // IDX_HD), blk(LANES, o4 // IDX_HD + 1), gain, gain]
                  + [pl.BlockSpec(memory_space=pl.ANY)] * len(states)),
        out_specs=[out(B_Q), state_spec(kv), out(B_KV), out(B_KV), out(I_Q), out(IDX_HD), out(LANES),
                   state_spec(kv), state_spec(1)],
        out_shape=[sds(B_Q, BF16), state_sds(kv), sds(B_KV, BF16), sds(B_KV, BF16),
                   sds(I_Q, BF16), sds(IDX_HD, BF16), sds(LANES, F32), state_sds(kv), state_sds(1)],
        input_output_aliases=({9: 1, 10: 7, 11: 8} if states else {}),
        compiler_params=_cparams(("parallel", "parallel")),
        name="dsa_post_" + kind,
    )(z, z, z, z, z, z, z, q_gain.reshape(1, B_HD).astype(F32), k_gain.reshape(1, B_HD).astype(F32), *states)


def _for_each_group(n, body):
    width = CHUNKS_PER_TRIP

    def group(jj, c):
        body(width * jj, width)
        return c

    lax.fori_loop(0, n // width, group, 0)
    done = (n // width) * width
    w = width // 2
    while w >= 1:
        take = (n - done) % (2 * w) >= w

        @pl.when(take)
        def _(done=done, w=w):
            body(done, w)
        done = done + jnp.where(take, w, 0)
        w //= 2


def _lane_fold(x, op):
    r = x[:, :LANES]
    for c in range(1, x.shape[1] // LANES):
        r = op(r, x[:, c * LANES:(c + 1) * LANES])
    return r


def _attend(streams, v_of, s_ref, mx_ref, ls_ref, acc_ref, qi, tq):
    mx_ref[...] = jnp.full(mx_ref.shape, NEG, F32)

    def score(j, count, delta=None):
        off = pl.multiple_of(j * tq, tq)
        n = count * tq
        r0 = 0
        for q, k_of, near_bias, far_bias in streams:
            rows = slice(r0, r0 + q.shape[0])
            r0 += q.shape[0]
            s = lax.dot_general(q, k_of(off, n), _NT, preferred_element_type=F32)
            b = far_bias(off, n) if delta is None else near_bias(delta, off)
            if b is not None:
                s = s + b
            s_ref[rows, pl.ds(off, n)] = s
            mx_ref[rows, :] = jnp.maximum(mx_ref[rows, :], _lane_fold(s, jnp.maximum))

    _for_each_group(jnp.maximum(qi - 1, 0), score)

    @pl.when(qi >= 1)
    def _():
        score(qi - 1, 1, 1)

    score(qi, 1, 0)

    m = jnp.max(mx_ref[...], axis=1, keepdims=True)
    mx_ref[...] = jnp.broadcast_to(m, mx_ref.shape)
    ls_ref[...] = jnp.zeros_like(ls_ref)
    acc_ref[...] = jnp.zeros_like(acc_ref)

    def pv(j, count):
        off = pl.multiple_of(j * tq, tq)
        n = count * tq
        s = s_ref[:, pl.ds(off, n)]
        m_rep = mx_ref[...]
        parts = [jnp.exp2(s[:, cc * LANES:(cc + 1) * LANES] - m_rep) for cc in range(n // LANES)]
        lsum = parts[0]
        for part in parts[1:]:
            lsum = lsum + part
        ls_ref[...] += lsum
        p = jnp.concatenate(parts, axis=1).astype(BF16)
        acc_ref[...] += jnp.dot(p, v_of(off, n), preferred_element_type=F32)

    _for_each_group(qi + 1, pv)
    return acc_ref[...] / jnp.sum(ls_ref[...], axis=1, keepdims=True)


def _diff_lambda(lp, lam_init):
    s1 = jnp.sum(lp[0:1] * lp[1:2], axis=1, keepdims=True)
    s2 = jnp.sum(lp[2:3] * lp[3:4], axis=1, keepdims=True)
    return jnp.exp(s1) - jnp.exp(s2) + lam_init


def _diff_attn_kernel(q_ref, k_ref, v_ref, bias_ref, lamp_ref, sub_ref, o_ref,
                      s_ref, mx_ref, ls_ref, acc_ref, *, lam_init, tq):
    qi = pl.program_id(2)
    streams = []
    for mp in range(2):
        sl = slice(mp * A_HD, (mp + 1) * A_HD)
        streams.append((q_ref[:, sl],
                        lambda off, n, sl=sl: k_ref[pl.ds(off, n), sl],
                        lambda delta, off, mp=mp: bias_ref[mp, delta * tq:(delta + 1) * tq, :],
                        lambda off, n: None))
    o2 = _attend(streams, lambda off, n: v_ref[pl.ds(off, n), :], s_ref, mx_ref, ls_ref, acc_ref, qi, tq)
    lam = _diff_lambda(lamp_ref[...], lam_init)
    o = o2[:tq] - lam * o2[tq:]
    o_ref[...] = (_head_rms(o, sub_ref[...]) * (1.0 - lam_init)).astype(o_ref.dtype)


def _diff_attn_prompt(q, kb, vb, bias, lam_p, subln, *, batch, t_pad, lam_init):
    tq = ATT_TILE
    nq = t_pad // tq
    hw = 2 * A_HD
    return pl.pallas_call(
        functools.partial(_diff_attn_kernel, lam_init=lam_init, tq=tq),
        grid=(batch, A_HEADS, nq),
        in_specs=[pl.BlockSpec((tq, hw), lambda b, h, i: (b * nq + i, h)),
                  pl.BlockSpec((t_pad, hw), lambda b, h, i: (b, h)),
                  pl.BlockSpec((t_pad, hw), lambda b, h, i: (b, h)),
                  pl.BlockSpec((2, 2 * tq, tq), lambda b, h, i: (h, 0, 0)),
                  pl.BlockSpec((4, A_HD), lambda b, h, i: (0, 0)),
                  pl.BlockSpec((1, hw), lambda b, h, i: (0, 0))],
        out_specs=pl.BlockSpec((tq, hw), lambda b, h, i: (b * nq + i, h)),
        out_shape=jax.ShapeDtypeStruct((batch * t_pad, A_QK), BF16),
        scratch_shapes=[pltpu.VMEM((2 * tq, t_pad), F32), pltpu.VMEM((2 * tq, LANES), F32),
                        pltpu.VMEM((2 * tq, LANES), F32), pltpu.VMEM((2 * tq, hw), F32)],
        compiler_params=_cparams(("parallel", "parallel", "arbitrary")),
        name="diff_attn_prompt",
    )(q, kb, vb, bias, lam_p.astype(F32), subln.reshape(1, hw).astype(F32))


def _sortable_key(x):
    bits = lax.bitcast_convert_type(x, I32)
    return bits ^ ((bits >> 31) & 0x7FFFFFFF)


def _kth_largest(count_ge, n_sel, shape, bits):
    zero = jnp.zeros(shape, I32)
    t = jnp.where(count_ge(zero) >= n_sel, zero, jnp.full(shape, -(1 << (bits - 1)), I32))

    def bit_body(i, t):
        cand = t + jnp.left_shift(jnp.int32(1), bits - 2 - i)
        return jnp.where(count_ge(cand) >= n_sel, cand, t)

    return lax.fori_loop(0, bits - 1, bit_body, t)


def _kth_largest_key(count_ge, n_sel, shape):
    return _kth_largest(count_ge, n_sel, shape, 32)


def _dsa_kernel(q_ref, iq_ref, iw_ref, k_ref, v_ref, ik_ref, bias_ref, o_ref,
                s_ref, mb_ref, hi_ref, lo_ref, mx_ref, ls_ref, acc_ref, *, tq, n_sel, t_real):
    qi = pl.program_id(1)
    n = pl.program_id(2)

    def keys_at(off):
        return lax.bitcast_convert_type(s_ref[0:tq, pl.ds(off, tq)], I32)

    @pl.when(n == 0)
    def _():
        row = lax.broadcasted_iota(I32, (tq, tq), 0)
        colm = lax.broadcasted_iota(I32, (tq, tq), 1)

        def score_body(j, c):
            off = pl.multiple_of(j * tq, tq)
            ikb = ik_ref[pl.ds(off, tq), :]
            acc = jnp.zeros((tq, tq), F32)
            for h in range(IDX_HEADS):
                d = lax.dot_general(iq_ref[:, h * IDX_HD:(h + 1) * IDX_HD], ikb, _NT,
                                    preferred_element_type=F32)
                acc = acc + jnp.maximum(d, 0.0) * iw_ref[:, h:h + 1]
            lim = jnp.where(j < qi, tq, 0)
            acc = jnp.where(colm <= row + lim, acc, -jnp.inf)
            key = _sortable_key(acc)
            s_ref[0:tq, pl.ds(off, tq)] = lax.bitcast_convert_type(key, F32)
            hi_ref[:, pl.ds(off, tq)] = (key >> 16).astype(I16)
            lo_ref[:, pl.ds(off, tq)] = ((key & 0xFFFF) - (1 << 15)).astype(I16)
            return c

        lax.fori_loop(0, qi + 1, score_body, 0)

        def count16(plane_ref, pred):
            def col_body(j, pc):
                blk = plane_ref[:, pl.ds(pl.multiple_of(j * tq, tq), tq)]
                ind = jnp.where(pred(blk), jnp.int16(1), jnp.int16(0))
                return pc + _lane_fold(ind, jnp.add)
            pc = lax.fori_loop(0, qi + 1, col_body, jnp.zeros((tq, LANES), I16))
            return jnp.sum(pc.astype(I32), axis=1, keepdims=True)

        t_hi = _kth_largest(lambda c: count16(hi_ref, lambda b: b >= c.astype(I16)), n_sel, (tq, 1), 16)
        t_hi16 = t_hi.astype(I16)
        need = n_sel - count16(hi_ref, lambda b: b > t_hi16)

        def tie_body(j, c):
            sl = pl.ds(pl.multiple_of(j * tq, tq), tq)
            lo_ref[:, sl] = jnp.where(hi_ref[:, sl] == t_hi16, lo_ref[:, sl], jnp.int16(-(1 << 15)))
            return c

        lax.fori_loop(0, qi + 1, tie_body, 0)
        t_lo = _kth_largest(lambda c: count16(lo_ref, lambda b: b >= c.astype(I16)), need, (tq, 1), 16)
        thr = (t_hi << 16) | (t_lo + (1 << 15))

        in_group = count16(hi_ref, lambda b: b == t_hi16)
        t_lo16 = t_lo.astype(I16)
        n_ge = jnp.where(t_lo == -(1 << 15), in_group, count16(lo_ref, lambda b: b >= t_lo16))
        real_row = qi * tq + row[:, 0:1] < t_real
        surplus = jnp.max(jnp.where(real_row, n_ge - need, 0)) > 0

        @pl.when(jnp.logical_not(surplus))
        def _():
            def mask_body(j, c):
                off = pl.multiple_of(j * tq, tq)
                mb_ref[:, pl.ds(off, tq)] = jnp.where(keys_at(off) >= thr, 0.0, NEG)
                return c

            lax.fori_loop(0, qi + 1, mask_body, 0)

        @pl.when(surplus)
        def _():
            def above_body(j, pc):
                key = keys_at(pl.multiple_of(j * tq, tq))
                return pc + _lane_fold(jnp.where(key > thr, 1, 0), jnp.add)

            above = lax.fori_loop(0, qi + 1, above_body, jnp.zeros((tq, LANES), I32))
            room = n_sel - jnp.sum(above, axis=1, keepdims=True)

            def rank_body(j, c):
                off = pl.multiple_of(j * tq, tq)
                rank = (1 << 15) - 1 - (colm + off)
                hi_ref[:, pl.ds(off, tq)] = jnp.where(keys_at(off) == thr, rank, -(1 << 15)).astype(I16)
                return c

            lax.fori_loop(0, qi + 1, rank_body, 0)
            r_last = _kth_largest(lambda c: count16(hi_ref, lambda b: b >= c.astype(I16)), room, (tq, 1), 16)
            col_last = (1 << 15) - 1 - r_last

            def mask_body(j, c):
                off = pl.multiple_of(j * tq, tq)
                key = keys_at(off)
                keep = (key > thr) | ((key == thr) & (colm + off <= col_last))
                mb_ref[:, pl.ds(off, tq)] = jnp.where(keep, 0.0, NEG)
                return c

            lax.fori_loop(0, qi + 1, mask_body, 0)

    hp = DSA_HEADS_PER_PASS
    for g0 in range(0, B_GROUP, hp):
        gs = range(g0, g0 + hp)
        q = jnp.concatenate([q_ref[:, g * B_HD:(g + 1) * B_HD] for g in gs], axis=0)

        def far_bias(off, n):
            mb = mb_ref[:, pl.ds(off, n)]
            return jnp.concatenate([mb] * hp, axis=0)

        def near_bias(delta, off, gs=gs):
            mb = mb_ref[:, pl.ds(off, tq)]
            return jnp.concatenate([mb + bias_ref[g, delta * tq:(delta + 1) * tq, :] for g in gs], axis=0)

        o = _attend([(q, lambda off, n: k_ref[pl.ds(off, n), :], near_bias, far_bias)],
                    lambda off, n: v_ref[pl.ds(off, n), :], s_ref, mx_ref, ls_ref, acc_ref, qi, tq)
        for i, g in enumerate(gs):
            o_ref[:, g * B_HD:(g + 1) * B_HD] = o[i * tq:(i + 1) * tq].astype(o_ref.dtype)


def _dsa_prompt(q, iq, iw, kb, vb, ikb, bias, *, batch, t_pad, t_real, n_sel):
    tq = ATT_TILE
    hp = DSA_HEADS_PER_PASS
    nq = t_pad // tq
    gw = B_GROUP * B_HD
    return pl.pallas_call(
        functools.partial(_dsa_kernel, tq=tq, n_sel=n_sel, t_real=t_real),
        grid=(batch, nq, B_KV_HEADS),
        in_specs=[pl.BlockSpec((tq, gw), lambda b, i, n: (b * nq + i, n)),
                  pl.BlockSpec((tq, I_Q), lambda b, i, n: (b * nq + i, 0)),
                  pl.BlockSpec((tq, LANES), lambda b, i, n: (b * nq + i, 0)),
                  pl.BlockSpec((t_pad, B_HD), lambda b, i, n: (b, n)),
                  pl.BlockSpec((t_pad, B_HD), lambda b, i, n: (b, n)),
                  pl.BlockSpec((t_pad, IDX_HD), lambda b, i, n: (b, 0)),
                  pl.BlockSpec((B_GROUP, 2 * tq, tq), lambda b, i, n: (n, 0, 0))],
        out_specs=pl.BlockSpec((tq, gw), lambda b, i, n: (b * nq + i, n)),
        out_shape=jax.ShapeDtypeStruct((batch * t_pad, B_Q), BF16),
        scratch_shapes=[pltpu.VMEM((hp * tq, t_pad), F32), pltpu.VMEM((tq, t_pad), F32),
                        pltpu.VMEM((tq, t_pad), I16), pltpu.VMEM((tq, t_pad), I16),
                        pltpu.VMEM((hp * tq, LANES), F32), pltpu.VMEM((hp * tq, LANES), F32),
                        pltpu.VMEM((hp * tq, B_HD), F32)],
        compiler_params=_cparams(("parallel", "arbitrary", "arbitrary"), DSA_VMEM_LIMIT_BYTES),
        name="dsa_prompt",
    )(q, iq, iw, kb, vb, ikb, bias)


def _page_specs(comps, layer, n_pages, npp):
    def spec(i):
        return pl.BlockSpec((1, 1, PAGE_SIZE * comps, LANES),
                            lambda b, p, pt, i=i: (layer, pt[b * n_pages + p * npp + i], 0, 0))
    return [spec(i) for i in range(npp)]


def _component(ref, c, comps):
    return ref[0, 0, pl.ds(c, PAGE_SIZE, stride=comps), :].astype(BF16)


def _sample_attn_body(p, npg, npp, qbd, k_refs, v_refs, kcomps, vcomps, v_ids, knew_ref, vnew_ref,
                      blast_ref, bnew_ref, m_ref, l_ref, acc_ref, mask_of):
    groups = len(v_ids)
    rg = LANES // groups
    vw = acc_ref.shape[1]

    @pl.when(p == 0)
    def _():
        m_ref[...] = jnp.full(m_ref.shape, NEG, F32)
        l_ref[...] = jnp.zeros_like(l_ref)
        acc_ref[...] = jnp.zeros_like(acc_ref)

    def scores(xcat, bias, mask_idx):
        s = lax.dot_general(qbd, xcat, _NT, preferred_element_type=F32)
        if bias is not None:
            s = s + bias
        return mask_of(mask_idx, s)

    def update(s_blocks, v_groups):
        s = jnp.concatenate(s_blocks, axis=1) if len(s_blocks) > 1 else s_blocks[0]
        m_old = m_ref[...]
        m_new = jnp.maximum(m_old, jnp.max(s, axis=1, keepdims=True))
        alpha = jnp.exp2(m_old - m_new)
        pr = jnp.exp2(s - m_new)
        l_ref[...] = alpha * l_ref[...] + jnp.sum(pr, axis=1, keepdims=True)
        m_ref[...] = m_new
        pb = pr.astype(BF16)
        for g in range(groups):
            rows = slice(g * rg, (g + 1) * rg)
            pv = None
            for i, v_group in enumerate(v_groups):
                d = jnp.dot(pb[rows, i * PAGE_SIZE:(i + 1) * PAGE_SIZE], v_group(g),
                            preferred_element_type=F32)
                pv = d if pv is None else pv + d
            acc_ref[rows, :] = alpha[rows] * acc_ref[rows, :] + pv

    is_last = p == npg - 1
    s_blocks, v_groups = [], []
    for i in range(npp):
        xcat = jnp.concatenate([_component(k_refs[i], c, kcomps) for c in range(kcomps)], axis=1)

        def v_group(g, i=i):
            parts = [_component(v_refs[i], c, vcomps) for c in v_ids[g]]
            return parts[0] if len(parts) == 1 else jnp.concatenate(parts, axis=1)

        bias = jnp.where(is_last, 1.0, 0.0) * blast_ref[...] if i == npp - 1 else None
        s_blocks.append(scores(xcat, bias, i))
        v_groups.append(v_group)
    update(s_blocks, v_groups)

    @pl.when(is_last)
    def _():
        update([scores(knew_ref[0], bnew_ref[...], npp)],
               [lambda g: vnew_ref[0, :, g * vw:(g + 1) * vw]])


def _diff_sample_kernel(pt_ref, q_ref, *refs, npp, npg, lam_init, n_new):
    k_refs = refs[:npp]
    v_refs = refs[npp:2 * npp]
    (knew_ref, vnew_ref, blast_ref, bnew_ref, lamp_ref, sub_ref,
     o_ref, m_ref, l_ref, acc_ref) = refs[2 * npp:]
    p = pl.program_id(1)
    comps = 2 * A_HEADS
    v_ids = [(h, A_HEADS + h) for h in range(A_HEADS)]
    _sample_attn_body(p, npg, npp, q_ref[0], k_refs, v_refs, comps, comps, v_ids, knew_ref, vnew_ref,
                      blast_ref, bnew_ref, m_ref, l_ref, acc_ref, lambda i, s: s)

    @pl.when(p == npg - 1)
    def _():
        lam = _diff_lambda(lamp_ref[...], lam_init)
        o = acc_ref[...] / l_ref[...]
        hw = 2 * A_HD
        for h in range(A_HEADS):
            r = 2 * h * SUBLANES
            od = o[r:r + SUBLANES] - lam * o[r + SUBLANES:r + 2 * SUBLANES]
            od = _head_rms(od, sub_ref[...]) * (1.0 - lam_init)
            o_ref[0, :, h * hw:(h + 1) * hw] = od[:n_new]


def _diff_attn_sample(qbd, cache_k, cache_v, pt_flat, knew, vnew, blast, bnew, lam_p, subln,
                      *, layer, n_pages, lam_init, n_new):
    db = qbd.shape[0]
    npp = _pages_per_step(n_pages, DIFF_SAMPLE_PAGES)
    npg = n_pages // npp
    hw = 2 * A_HD
    comps = 2 * A_HEADS
    per_b = lambda shape: pl.BlockSpec((1,) + shape, lambda b, p, pt: (b, 0, 0))
    const = lambda shape: pl.BlockSpec(shape, lambda b, p, pt: (0, 0))
    grid_spec = pltpu.PrefetchScalarGridSpec(
        num_scalar_prefetch=1,
        grid=(db, npg),
        in_specs=([per_b((LANES, A_QK))] + _page_specs(comps, layer, n_pages, npp)
                  + _page_specs(comps, layer, n_pages, npp)
                  + [per_b((PAGE_SIZE, A_QK)), per_b((PAGE_SIZE, A_QK)),
                     const((LANES, PAGE_SIZE)), const((LANES, PAGE_SIZE)),
                     const((4, A_HD)), const((1, hw))]),
        out_specs=pl.BlockSpec((1, n_new, A_QK), lambda b, p, pt: (b, 0, 0)),
        scratch_shapes=[pltpu.VMEM((LANES, 1), F32), pltpu.VMEM((LANES, 1), F32),
                        pltpu.VMEM((LANES, hw), F32)],
    )
    return pl.pallas_call(
        functools.partial(_diff_sample_kernel, npp=npp, npg=npg, lam_init=lam_init, n_new=n_new),
        grid_spec=grid_spec,
        out_shape=jax.ShapeDtypeStruct((db, n_new, A_QK), F32),
        compiler_params=_cparams(("parallel", "arbitrary")),
        name="diff_attn_sample",
    )(pt_flat, qbd, *([cache_k] * npp), *([cache_v] * npp), knew, vnew, blast, bnew,
      lam_p.astype(F32), subln.reshape(1, hw).astype(F32))


def _idx_sample_kernel(pt_ref, iq_ref, w_ref, *refs, npp, npg, n_pages, n_sel, n_new):
    ik_refs = refs[:npp]
    iknew_ref, keys_ref, thr_ref = refs[npp:]
    p = pl.program_id(1)
    iq = iq_ref[0]
    w = w_ref[0]

    def slot_scores(ik):
        d = lax.dot_general(iq, ik, _NT, preferred_element_type=F32)
        r = jnp.maximum(d, 0.0) * w
        return jnp.sum(r.reshape(SUBLANES, IDX_HEADS, PAGE_SIZE), axis=1)

    for i in range(npp):
        sc = slot_scores(ik_refs[i][0, 0].astype(BF16))
        off = pl.multiple_of((p * npp + i) * PAGE_SIZE, PAGE_SIZE)
        keys_ref[0, :, pl.ds(off, PAGE_SIZE)] = _sortable_key(sc)

    @pl.when(p == npg - 1)
    def _():
        sc = slot_scores(iknew_ref[0])
        srow = lax.broadcasted_iota(I32, sc.shape, 0)
        tcol = lax.broadcasted_iota(I32, sc.shape, 1)
        sc = jnp.where((tcol <= srow) & (tcol < n_new), sc, -jnp.inf)
        keys_ref[0, :, n_pages * PAGE_SIZE:(n_pages + 1) * PAGE_SIZE] = _sortable_key(sc)

        def count_ge(cand):
            return jnp.sum(jnp.where(keys_ref[0] >= cand, 1, 0), axis=1, keepdims=True)

        t = _kth_largest_key(count_ge, n_sel, (SUBLANES, 1))
        thr_ref[0] = jnp.broadcast_to(t, (SUBLANES, LANES))

        @pl.when(jnp.max(count_ge(t)[:n_new]) > n_sel)
        def _():
            keys = keys_ref[0]
            col = lax.broadcasted_iota(I32, keys.shape, 1)
            room = n_sel - jnp.sum(jnp.where(keys > t, 1, 0), axis=1, keepdims=True)
            tie = keys == t
            rank = jnp.where(tie, (1 << 15) - 1 - col, -(1 << 15))
            r_last = _kth_largest(lambda c: jnp.sum(jnp.where(rank >= c, 1, 0), axis=1, keepdims=True),
                                  room, (SUBLANES, 1), 16)
            col_last = (1 << 15) - 1 - r_last
            keys_ref[0] = jnp.where(tie & (col > col_last), keys - 1, keys)


def _idx_sample(iqs, ws, cache_ki, pt_flat, iknew, *, layer, n_pages, n_sel, n_new):
    db = iqs.shape[0]
    npp = _pages_per_step(n_pages, IDX_SAMPLE_PAGES)
    npg = n_pages // npp
    width = (n_pages + 1) * PAGE_SIZE
    per_b = lambda shape: pl.BlockSpec((1,) + shape, lambda b, p, pt: (b, 0, 0))
    grid_spec = pltpu.PrefetchScalarGridSpec(
        num_scalar_prefetch=1,
        grid=(db, npg),
        in_specs=([per_b((SUBLANES * IDX_HEADS, IDX_HD)), per_b((SUBLANES * IDX_HEADS, 1))]
                  + _page_specs(1, layer, n_pages, npp) + [per_b((PAGE_SIZE, IDX_HD))]),
        out_specs=[per_b((SUBLANES, width)), per_b((SUBLANES, LANES))],
    )
    return pl.pallas_call(
        functools.partial(_idx_sample_kernel, npp=npp, npg=npg, n_pages=n_pages, n_sel=n_sel, n_new=n_new),
        grid_spec=grid_spec,
        out_shape=[jax.ShapeDtypeStruct((db, SUBLANES, width), I32),
                   jax.ShapeDtypeStruct((db, SUBLANES, LANES), I32)],
        compiler_params=_cparams(("parallel", "arbitrary")),
        name="dsa_index_sample",
    )(pt_flat, iqs, ws, *([cache_ki] * npp), iknew)


def _dsa_sample_kernel(pt_ref, q_ref, *refs, npp, npg, n_new):
    k_refs = refs[:npp]
    v_refs = refs[npp:2 * npp]
    key_refs = refs[2 * npp:3 * npp + 1]
    (thr_ref, knew_ref, vnew_ref, blast_ref, bnew_ref,
     o_ref, m_ref, l_ref, acc_ref) = refs[3 * npp + 1:]
    p = pl.program_id(1)

    def mask_of(i, s):
        keys = jnp.concatenate([key_refs[i][0]] * B_HEADS, axis=0)
        thr = jnp.concatenate([thr_ref[0]] * B_HEADS, axis=0)
        return jnp.where(keys >= thr, s, NEG)

    v_ids = [(n,) for n in range(B_KV_HEADS)]
    _sample_attn_body(p, npg, npp, q_ref[0], k_refs, v_refs, B_KV_HEADS, B_KV_HEADS, v_ids, knew_ref,
                      vnew_ref, blast_ref, bnew_ref, m_ref, l_ref, acc_ref, mask_of)

    @pl.when(p == npg - 1)
    def _():
        o = acc_ref[...] / l_ref[...]
        for hd in range(B_HEADS):
            o_ref[0, :, hd * B_HD:(hd + 1) * B_HD] = o[hd * SUBLANES:hd * SUBLANES + n_new]


def _dsa_attn_sample(qbd, cache_k, cache_v, keys, thr, pt_flat, knew, vnew, blast, bnew,
                     *, layer, n_pages, n_new):
    db = qbd.shape[0]
    npp = _pages_per_step(n_pages, DSA_SAMPLE_PAGES)
    npg = n_pages // npp
    per_b = lambda shape: pl.BlockSpec((1,) + shape, lambda b, p, pt: (b, 0, 0))
    const = lambda shape: pl.BlockSpec(shape, lambda b, p, pt: (0, 0))
    key_specs = [pl.BlockSpec((1, SUBLANES, PAGE_SIZE), lambda b, p, pt, i=i: (b, 0, p * npp + i))
                 for i in range(npp)]
    key_specs.append(pl.BlockSpec((1, SUBLANES, PAGE_SIZE), lambda b, p, pt: (b, 0, n_pages)))
    grid_spec = pltpu.PrefetchScalarGridSpec(
        num_scalar_prefetch=1,
        grid=(db, npg),
        in_specs=([per_b((LANES, B_KV))] + _page_specs(B_KV_HEADS, layer, n_pages, npp)
                  + _page_specs(B_KV_HEADS, layer, n_pages, npp) + key_specs
                  + [per_b((SUBLANES, LANES)), per_b((PAGE_SIZE, B_KV)), per_b((PAGE_SIZE, B_KV)),
                     const((LANES, PAGE_SIZE)), const((LANES, PAGE_SIZE))]),
        out_specs=pl.BlockSpec((1, n_new, B_Q), lambda b, p, pt: (b, 0, 0)),
        scratch_shapes=[pltpu.VMEM((LANES, 1), F32), pltpu.VMEM((LANES, 1), F32),
                        pltpu.VMEM((LANES, B_HD), F32)],
    )
    return pl.pallas_call(
        functools.partial(_dsa_sample_kernel, npp=npp, npg=npg, n_new=n_new),
        grid_spec=grid_spec,
        out_shape=jax.ShapeDtypeStruct((db, n_new, B_Q), F32),
        compiler_params=_cparams(("parallel", "arbitrary")),
        name="dsa_attn_sample",
    )(pt_flat, qbd, *([cache_k] * npp), *([cache_v] * npp), *([keys] * (npp + 1)), thr, knew, vnew,
      blast, bnew)


def _pad_rows(x, rows):
    return jnp.pad(x, ((0, 0), (0, rows - x.shape[1]), (0, 0)))


def kernel(x_prompt, x_sample, cache_a_k, cache_a_v, cache_b_k, cache_b_v, cache_b_kidx, page_table, meta_tokens, rel_bias_table, norm_mix, norm_ffn, a_w_in, a_w_out, a_q_norm, a_k_norm, a_lambda, a_subln, b_w_in, b_w_out, b_q_norm, b_k_norm, ffn_w1, ffn_w2):
    batch, seq, d_model = x_prompt.shape
    db, ds, _ = x_sample.shape
    depth = norm_mix.shape[0]
    n_pages = page_table.shape[1]
    past_len = n_pages * PAGE_SIZE
    t_real = seq + N_META
    t_pad = -(-t_real // Q_BLOCK) * Q_BLOCK
    tq = ATT_TILE
    n_s = db * ds
    r0 = batch * t_pad
    m_tot = r0 + TAIL_ROWS
    assert t_pad % tq == 0 and r0 % TAIL_ROWS == 0 and n_s <= TAIL_ROWS and ds <= SUBLANES
    n_sel_prompt = min(TOPK_MAX, seq // 4)
    n_sel_sample = min(TOPK_MAX, (past_len + ds) // 4)

    meta = jnp.broadcast_to(meta_tokens.astype(F32), (batch, N_META, d_model))
    xp = jnp.concatenate([meta, x_prompt, jnp.zeros((batch, t_pad - t_real, d_model), F32)], axis=1)
    x = jnp.concatenate([xp.reshape(r0, d_model), x_sample.reshape(n_s, d_model),
                         jnp.zeros((TAIL_ROWS - n_s, d_model), F32)], axis=0)

    ri = lax.broadcasted_iota(I32, (tq, tq), 0)
    ci = lax.broadcasted_iota(I32, (tq, tq), 1)
    r8 = lax.broadcasted_iota(I32, (SUBLANES, tq), 0)
    c8 = lax.broadcasted_iota(I32, (SUBLANES, tq), 1)
    dist = jnp.concatenate([ri - ci, tq + ri - ci, PAGE_SIZE + r8 - c8, r8 - c8], axis=0)
    bias_all = _bias_tiles(rel_bias_table, dist)
    bias_prompt = bias_all[:, :2 * tq]
    b_last = bias_all[:, 2 * tq:2 * tq + SUBLANES, :PAGE_SIZE].reshape(N_BIAS_HEADS * SUBLANES, PAGE_SIZE)
    b_new = bias_all[:, 2 * tq + SUBLANES:2 * tq + 2 * SUBLANES, :PAGE_SIZE].reshape(
        N_BIAS_HEADS * SUBLANES, PAGE_SIZE)

    pt_flat = page_table.reshape(-1).astype(I32)
    n_a, n_pool = cache_a_k.shape[:2]
    n_b = cache_b_k.shape[0]
    cache_a_k2 = cache_a_k.reshape(n_a, n_pool, PAGE_SIZE * 2 * A_HEADS, A_HD)
    cache_a_v2 = cache_a_v.reshape(n_a, n_pool, PAGE_SIZE, A_HEADS, 2, A_HD).transpose(
        0, 1, 2, 4, 3, 5).reshape(n_a, n_pool, PAGE_SIZE * 2 * A_HEADS, A_HD)
    cache_b_k2 = cache_b_k.reshape(n_b, n_pool, PAGE_SIZE * B_KV_HEADS, B_HD)
    cache_b_v2 = cache_b_v.reshape(n_b, n_pool, PAGE_SIZE * B_KV_HEADS, B_HD)

    b_in_pad = -(-B_IN // 256) * 256
    wb_in = jnp.pad(b_w_in, ((0, 0), (0, 0), (0, b_in_pad - B_IN))).astype(BF16)
    w_ffn2 = ffn_w2.astype(BF16)

    def mm(a, w, layer, **kw):
        if a.shape[1] <= RESIDENT_MAX_K and kw.get("mode") != "res":
            return _matmul_resident(a, w, layer, **kw)
        return _matmul(a, w.astype(BF16), layer, **kw)
    eye_m = jnp.eye(2 * A_HEADS, dtype=BF16)
    eye_n = jnp.eye(B_KV_HEADS, dtype=BF16)

    def rows_of(a):
        return a[:n_s].reshape((db, ds) + a.shape[1:])

    n_a_layers = (depth + N_MIXERS - 1) // N_MIXERS
    n_b_layers = depth // N_MIXERS
    a_states, b_states = (), ()
    outs = {k: [] for k in ("aks", "avs", "bks", "bvs", "bis")}
    for i in range(depth):
        j = i // N_MIXERS
        h = _rmsnorm(x, norm_mix[i])
        if i % N_MIXERS == 0:
            lam_init = 0.8 - 0.6 * math.exp(-0.3 * i)
            z = mm(h, a_w_in, j)
            dims = (batch, t_pad, t_real, r0, n_a_layers, j)
            q, k_st, kb, vb, v_st = _a_post(z, a_q_norm[j], a_k_norm[j], "prompt", dims, a_states)
            a_states = (k_st, v_st)
            if t_real % POST_TILE:
                _, k_st, _, _, v_st = _a_post(z, a_q_norm[j], a_k_norm[j], "edge", dims, a_states)
                a_states = (k_st, v_st)
            q_t, k_t, kb_t, vb_t, v_t = _a_post(z, a_q_norm[j], a_k_norm[j], "tail", dims)
            o_p = _diff_attn_prompt(q, kb, vb, bias_prompt, a_lambda[j], a_subln[j],
                                    batch=batch, t_pad=t_pad, lam_init=lam_init)
            qs = _pad_rows(rows_of(q_t), SUBLANES).reshape(db, SUBLANES, 2 * A_HEADS, A_HD)
            qbd = (qs[:, None] * eye_m[None, :, None, :, None]).reshape(db, LANES, A_QK)
            o_s = _diff_attn_sample(qbd, cache_a_k2, cache_a_v2, pt_flat,
                                    _pad_rows(rows_of(kb_t), PAGE_SIZE), _pad_rows(rows_of(vb_t), PAGE_SIZE),
                                    b_last, b_new, a_lambda[j], a_subln[j],
                                    layer=j, n_pages=n_pages, lam_init=lam_init, n_new=ds)
            w_out = a_w_out
            outs["aks"].append(rows_of(k_t.reshape(TAIL_ROWS, A_HEADS, 2, A_HD)))
            outs["avs"].append(rows_of(v_t.reshape(TAIL_ROWS, 2, A_HEADS, A_HD).transpose(0, 2, 1, 3)
                                       .reshape(TAIL_ROWS, A_HEADS, 2 * A_HD)))
        else:
            z = _matmul(h, wb_in, j)
            dims = (batch, t_pad, t_real, r0, n_b_layers, j)
            q, k_st, kb, vb, iq, ikb, iw, v_st, ik_st = _b_post(z, b_q_norm[j], b_k_norm[j], "prompt", dims,
                                                                b_states)
            b_states = (k_st, v_st, ik_st)
            if t_real % POST_TILE:
                edge = _b_post(z, b_q_norm[j], b_k_norm[j], "edge", dims, b_states)
                b_states = (edge[1], edge[7], edge[8])
            q_t, k_t, kb_t, vb_t, iq_t, ikb_t, iw_t, v_t, ik_t = _b_post(z, b_q_norm[j], b_k_norm[j], "tail", dims)
            o_p = _dsa_prompt(q, iq, iw, kb, vb, ikb, bias_prompt,
                              batch=batch, t_pad=t_pad, t_real=t_real, n_sel=n_sel_prompt)
            iqs = _pad_rows(rows_of(iq_t), SUBLANES).reshape(db, SUBLANES * IDX_HEADS, IDX_HD)
            ws = _pad_rows(rows_of(iw_t)[:, :, :IDX_HEADS], SUBLANES).reshape(db, SUBLANES * IDX_HEADS, 1)
            keys, thr = _idx_sample(iqs, ws, cache_b_kidx, pt_flat, _pad_rows(rows_of(ikb_t), PAGE_SIZE),
                                    layer=j, n_pages=n_pages, n_sel=n_sel_sample, n_new=ds)
            qs = _pad_rows(rows_of(q_t), SUBLANES).reshape(db, SUBLANES, B_KV_HEADS, B_GROUP, B_HD)
            qbd = (qs.transpose(0, 3, 1, 2, 4)[:, None]
                   * eye_n[None, :, None, None, :, None]).reshape(db, LANES, B_KV)
            o_s = _dsa_attn_sample(qbd, cache_b_k2, cache_b_v2, keys, thr, pt_flat,
                                   _pad_rows(rows_of(kb_t), PAGE_SIZE), _pad_rows(rows_of(vb_t), PAGE_SIZE),
                                   b_last, b_new, layer=j, n_pages=n_pages, n_new=ds)
            w_out = b_w_out
            outs["bks"].append(rows_of(k_t.reshape(TAIL_ROWS, B_KV_HEADS, B_HD)))
            outs["bvs"].append(rows_of(v_t.reshape(TAIL_ROWS, B_KV_HEADS, B_HD)))
            outs["bis"].append(rows_of(ik_t))
        width = o_p.shape[1]
        o = jnp.concatenate([o_p, o_s.reshape(n_s, width).astype(BF16),
                             jnp.zeros((TAIL_ROWS - n_s, width), BF16)], axis=0)
        x = mm(o, w_out, j, mode="res", res=x)
        hf = _rmsnorm(x, norm_ffn[i])
        u = mm(hf, ffn_w1, i, mode="relu2", out_dtype=BF16)
        x = _matmul(u, w_ffn2, i, mode="res", res=x)

    y_prompt = x[:r0].reshape(batch, t_pad, d_model)[:, N_META:t_real]
    y_sample = x[r0:r0 + n_s].reshape(db, ds, d_model)
    st = lambda key: jnp.stack(outs[key])
    a_k_p = a_states[0].reshape(n_a_layers, batch, t_real, A_HEADS, 2, A_HD)
    a_v_p = a_states[1].reshape(n_a_layers, batch, t_real, 2, A_HEADS, A_HD).transpose(
        0, 1, 2, 4, 3, 5).reshape(n_a_layers, batch, t_real, A_HEADS, 2 * A_HD)
    b_k_p = b_states[0].reshape(n_b_layers, batch, t_real, B_KV_HEADS, B_HD)
    b_v_p = b_states[1].reshape(n_b_layers, batch, t_real, B_KV_HEADS, B_HD)
    return (y_prompt, y_sample, a_k_p, a_v_p, b_k_p, b_v_p, b_states[2],
            st("aks"), st("avs"), st("bks"), st("bvs"), st("bis"))
```
